```python
import math
import jax, jax.numpy as jnp
from jax import lax
import numpy as np

D_MODEL = 1024
BATCH = 8
SEQ = 4096
DEPTH = 4

N_MIXERS = 2
SB_HEADS = 16
SB_HEAD_DIM = D_MODEL // SB_HEADS
Q_BLOCK = 128
HG_EXPAND = 128
HG_HEADS = D_MODEL // HG_EXPAND
HG_KEY_DIM = HG_EXPAND
HG_VAL_DIM = D_MODEL // HG_HEADS
HG_CHUNK = 64
D_FF = 4 * D_MODEL
N_SB = (DEPTH + N_MIXERS - 1) // N_MIXERS
N_HG = DEPTH // N_MIXERS
EPS = 1e-6

kernel_name = "stick_breaking_hgrn2_hybrid"


def rmsnorm(x, gain):
    xf = x.astype(jnp.float32)
    y = xf * lax.rsqrt(jnp.mean(xf * xf, axis=-1, keepdims=True) + EPS)
    return (y * gain.astype(jnp.float32)).astype(x.dtype)


def stick_breaking_attention(q, k, v):
    seq = q.shape[2]
    scale = 1.0 / math.sqrt(q.shape[-1])
    outs = []
    for t0 in range(0, seq, Q_BLOCK):
        t1 = t0 + Q_BLOCK
        z = jnp.einsum('bhqd,bhkd->bhqk', q[:, :, t0:t1], k[:, :, :t1]).astype(jnp.float32) * scale
        causal = jnp.arange(t1)[None, :] < (t0 + jnp.arange(Q_BLOCK))[:, None]
        log_stay = jnp.where(causal, jax.nn.log_sigmoid(-z), 0.0)
        log_between = lax.cumsum(log_stay, axis=3, reverse=True) - log_stay
        weights = jnp.where(causal, jnp.exp(jax.nn.log_sigmoid(z) + log_between), 0.0)
        outs.append(jnp.einsum('bhqk,bhkd->bhqd', weights.astype(v.dtype), v[:, :, :t1]))
    return jnp.concatenate(outs, axis=2)


def stick_breaking_mixer(h, w_qkv, q_gain, k_gain, w_o):
    bsz, seq, _ = h.shape
    qkv = h @ w_qkv
    q, k, v = jnp.split(qkv, 3, axis=-1)
    q = rmsnorm(q.reshape(bsz, seq, SB_HEADS, SB_HEAD_DIM), q_gain)
    k = rmsnorm(k.reshape(bsz, seq, SB_HEADS, SB_HEAD_DIM), k_gain)
    v = v.reshape(bsz, seq, SB_HEADS, SB_HEAD_DIM)
    q, k, v = (jnp.transpose(a, (0, 2, 1, 3)) for a in (q, k, v))
    o = stick_breaking_attention(q, k, v)
    o = jnp.transpose(o, (0, 2, 1, 3)).reshape(bsz, seq, D_MODEL)
    return o @ w_o


def hgrn2_chunk_scan(q, k, v, log_f):
    bsz, nh, seq, dk = q.shape
    dv = v.shape[-1]
    n_chunks = seq // HG_CHUNK

    def to_chunks(a):
        return jnp.moveaxis(a.astype(jnp.float32).reshape(bsz, nh, n_chunks, HG_CHUNK, a.shape[-1]), 2, 0)

    incl = jnp.tril(jnp.ones((HG_CHUNK, HG_CHUNK), dtype=bool))

    def step(state, inp):
        qc, kc, vc, lfc = inp
        b = jnp.cumsum(lfc, axis=2)
        inter = jnp.einsum('bhck,bhkv->bhcv', qc * jnp.exp(b), state)
        diff = b[:, :, :, None, :] - b[:, :, None, :, :]
        decay = jnp.where(incl[:, :, None], jnp.exp(jnp.minimum(diff, 0.0)), 0.0)
        scores = jnp.einsum('bhtk,bhsk,bhtsk->bhts', qc, kc, decay)
        intra = jnp.einsum('bhts,bhsv->bhtv', scores, vc)
        b_last = b[:, :, -1:, :]
        new_state = (jnp.exp(b_last[:, :, 0, :])[..., None] * state
                     + jnp.einsum('bhsk,bhsv->bhkv', kc * jnp.exp(b_last - b), vc))
        return new_state, inter + intra

    init = jnp.zeros((bsz, nh, dk, dv), jnp.float32)
    _, ys = lax.scan(step, init, (to_chunks(q), to_chunks(k), to_chunks(v), to_chunks(log_f)))
    return jnp.moveaxis(ys, 0, 2).reshape(bsz, nh, seq, dv)


def hgrn2_mixer(h, w_in, lower_bound, norm_gain, w_o):
    bsz, seq, _ = h.shape
    proj = h @ w_in
    q, f, i, g = jnp.split(proj, 4, axis=-1)
    q = jax.nn.silu(q)
    lb = lower_bound.astype(jnp.float32)
    forget = lb + (1.0 - lb) * jax.nn.sigmoid(f.astype(jnp.float32))
    log_f = jnp.log(forget)
    k = -jnp.expm1(log_f)

    def heads(a, d):
        return jnp.transpose(a.reshape(bsz, seq, HG_HEADS, d), (0, 2, 1, 3))

    o = hgrn2_chunk_scan(heads(q, HG_KEY_DIM), heads(k, HG_KEY_DIM),
                         heads(i, HG_VAL_DIM), heads(log_f, HG_KEY_DIM))
    o = jnp.transpose(o, (0, 2, 1, 3)).astype(h.dtype)
    o = rmsnorm(o, norm_gain).reshape(bsz, seq, D_MODEL)
    o = o * jax.nn.sigmoid(g)
    return o @ w_o


def squared_relu_mlp(h, w1, w2):
    a = jax.nn.relu(h @ w1)
    return (a * a) @ w2


def setup_inputs(seed: int = 0) -> dict:
    key = jax.random.key(seed)
    ks = jax.random.split(key, 12)
    d_in = D_MODEL ** -0.5
    res = (2 * DEPTH) ** -0.5
    nrm = jax.random.normal
    return {
        "x": nrm(ks[0], (BATCH, SEQ, D_MODEL), jnp.float32),
        "norm_gains": 1.0 + 0.02 * nrm(ks[1], (DEPTH, 2, D_MODEL), jnp.float32),
        "sb_w_qkv": nrm(ks[2], (N_SB, D_MODEL, 3 * D_MODEL), jnp.float32) * d_in,
        "sb_q_gain": 1.0 + 0.02 * nrm(ks[3], (N_SB, SB_HEAD_DIM), jnp.float32),
        "sb_k_gain": 1.0 + 0.02 * nrm(ks[4], (N_SB, SB_HEAD_DIM), jnp.float32),
        "sb_w_o": nrm(ks[5], (N_SB, D_MODEL, D_MODEL), jnp.float32) * d_in * res,
        "hg_w_in": nrm(ks[6], (N_HG, D_MODEL, 4 * D_MODEL), jnp.float32) * d_in,
        "hg_lb_logits": 0.5 * nrm(ks[7], (N_HG, D_MODEL), jnp.float32),
        "hg_norm_gain": 1.0 + 0.02 * nrm(ks[8], (N_HG, HG_VAL_DIM), jnp.float32),
        "hg_w_o": nrm(ks[9], (N_HG, D_MODEL, D_MODEL), jnp.float32) * d_in * res,
        "mlp_w1": nrm(ks[10], (DEPTH, D_MODEL, D_FF), jnp.float32) * d_in,
        "mlp_w2": nrm(ks[11], (DEPTH, D_FF, D_MODEL), jnp.float32) * (D_FF ** -0.5) * res,
    }


def reference(x, norm_gains, sb_w_qkv, sb_q_gain, sb_k_gain, sb_w_o,
              hg_w_in, hg_lb_logits, hg_norm_gain, hg_w_o, mlp_w1, mlp_w2):
    p = jax.nn.softmax(hg_lb_logits.astype(jnp.float32), axis=0)
    lower_bounds = jnp.cumsum(p, axis=0) - p[0:1]
    for layer in range(DEPTH):
        j = layer // N_MIXERS
        h = rmsnorm(x, norm_gains[layer, 0])
        if layer % N_MIXERS == 0:
            x = x + stick_breaking_mixer(h, sb_w_qkv[j], sb_q_gain[j], sb_k_gain[j], sb_w_o[j])
        else:
            x = x + hgrn2_mixer(h, hg_w_in[j], lower_bounds[j], hg_norm_gain[j], hg_w_o[j])
        h = rmsnorm(x, norm_gains[layer, 1])
        x = x + squared_relu_mlp(h, mlp_w1[layer], mlp_w2[layer])
    return x
```

```python
import functools
import math

import jax
import jax.numpy as jnp
from jax import lax
from jax.experimental import pallas as pl
from jax.experimental.pallas import tpu as pltpu

F32 = jnp.float32
BF16 = jnp.bfloat16

D_MODEL = 1024
D_FF = 4 * D_MODEL
EPS = 1e-6
SB_HEAD_DIM = 64
HG_HEAD_DIM = 128
LANES = 128
VMEM_LIMIT_BYTES = 56 * 1024 * 1024

TOKEN_TILE = 512
COL_CHUNK = 1024

ATT_BLOCK = 128
ATT_SKIP_LOG = -90.0

HG_CHUNK = 64
HG_SUB = 16


def _const_spec(shape):
    return pl.BlockSpec(shape, lambda *_: (0,) * len(shape), pipeline_mode=pl.Buffered(1))


def _rmsnorm(x, gain):
    ms = jnp.mean(x * x, axis=-1, keepdims=True)
    return x * lax.rsqrt(ms + EPS) * gain


def _split2(x):
    hi = x.astype(BF16)
    lo = (x - hi.astype(F32)).astype(BF16)
    return hi, lo


def _split3(x):
    h1 = x.astype(BF16)
    r1 = x - h1.astype(F32)
    h2 = r1.astype(BF16)
    h3 = (r1 - h2.astype(F32)).astype(BF16)
    return h1, h2, h3


def _dot(a, b):
    return jnp.dot(a, b, preferred_element_type=F32)


def _dot_nt(a, b):
    return lax.dot_general(a, b, (((1,), (1,)), ((), ())), preferred_element_type=F32)


def _dot_tn(a, b):
    return lax.dot_general(a, b, (((0,), (0,)), ((), ())), preferred_element_type=F32)


def _sb_proj_kernel(x_ref, g_ref, w_ref, o_ref):
    h = _rmsnorm(x_ref[...], g_ref[...]).astype(BF16)
    for c in range(o_ref.shape[1] // COL_CHUNK):
        cols = slice(c * COL_CHUNK, (c + 1) * COL_CHUNK)
        o_ref[:, cols] = _dot(h, w_ref[:, cols]).astype(BF16)


def _sb_proj(x, gain, w):
    n = x.shape[0]
    dout = w.shape[1]
    return pl.pallas_call(
        _sb_proj_kernel,
        grid=(n // TOKEN_TILE,),
        in_specs=[
            pl.BlockSpec((TOKEN_TILE, D_MODEL), lambda i: (i, 0)),
            _const_spec((1, D_MODEL)),
            _const_spec((D_MODEL, dout)),
        ],
        out_specs=pl.BlockSpec((TOKEN_TILE, dout), lambda i: (i, 0)),
        out_shape=jax.ShapeDtypeStruct((n, dout), BF16),
        compiler_params=pltpu.CompilerParams(
            dimension_semantics=("arbitrary",), vmem_limit_bytes=VMEM_LIMIT_BYTES),
        name="sb_proj",
    )(x, gain, w)


def _sb_attn_kernel(q_ref, k_ref, v_ref, qg_ref, kg_ref, o_ref,
                    kn_ref, vcat_ref, uu_ref, carry_ref, acc_ref, *, seq):
    nblk = seq // ATT_BLOCK
    blk = ATT_BLOCK
    lane = lax.broadcasted_iota(jnp.int32, (1, LANES), 1)
    is_a = lane < SB_HEAD_DIM

    r = lax.broadcasted_iota(jnp.int32, (2 * LANES, LANES), 0)
    c = lax.broadcasted_iota(jnp.int32, (2 * LANES, LANES), 1)
    group_ones = (((r & (LANES - 1)) >> 6) == (c >> 6)).astype(BF16)

    def head_norm(xf, gain):
        hi, lo = _split2(xf * xf)
        ss = _dot(jnp.concatenate([hi, lo], axis=1), group_ones)
        return xf * lax.rsqrt(ss * (1.0 / SB_HEAD_DIM) + EPS) * gain

    r = lax.broadcasted_iota(jnp.int32, (2 * blk, 2 * blk), 0) & (blk - 1)
    c = lax.broadcasted_iota(jnp.int32, (2 * blk, 2 * blk), 1)
    uu_ref[...] = ((c >= blk) | (r > c)).astype(BF16)

    def kprep(i, _):
        rows = pl.ds(pl.multiple_of(i * 512, 512), 512)
        kn_ref[rows, :] = head_norm(k_ref[0, rows, :].astype(F32), kg_ref[...]).astype(BF16)
        return 0
    lax.fori_loop(0, seq // 512, kprep, 0)

    def vprep(j, _):
        rows = pl.ds(pl.multiple_of(j * blk, blk), blk)
        vb = v_ref[0, rows, :].astype(F32)
        vcat_ref[j, 0:blk, :] = jnp.where(is_a, vb, 0.0).astype(BF16)
        vcat_ref[j, blk:2 * blk, :] = jnp.where(is_a, 0.0, vb).astype(BF16)
        return 0
    lax.fori_loop(0, nblk, vprep, 0)

    row2 = lax.broadcasted_iota(jnp.int32, (2 * blk, blk), 0) & (blk - 1)
    col2 = lax.broadcasted_iota(jnp.int32, (2 * blk, blk), 1)
    causal = col2 < row2

    def tile(qq, j, diag):
        rows = pl.ds(pl.multiple_of(j * blk, blk), blk)
        z = _dot_nt(qq, kn_ref[rows, :])
        log_stay = -(jnp.maximum(z, 0.0) + jnp.log(1.0 + jnp.exp(-jnp.abs(z))))
        if diag:
            log_stay = jnp.where(causal, log_stay, 0.0)
        hi, lo = _split2(log_stay)
        cs = _dot(jnp.concatenate([hi, lo], axis=1), uu_ref[...])
        carry = carry_ref[...]
        w = jnp.exp(z + log_stay + cs[:, :blk] + carry)
        if diag:
            w = jnp.where(causal, w, 0.0)
        carry_ref[...] = carry + cs[:, blk:]
        wb = w.astype(BF16)
        wcat = jnp.concatenate([wb[:blk], wb[blk:]], axis=1)
        acc_ref[...] += _dot(wcat, vcat_ref[j])

    scale = 1.0 / math.sqrt(SB_HEAD_DIM)

    def qblock(qi, _):
        rows = pl.ds(pl.multiple_of(qi * blk, blk), blk)
        qn = head_norm(q_ref[0, rows, :].astype(F32), qg_ref[...]) * scale
        qq = jnp.concatenate([jnp.where(is_a, qn, 0.0), jnp.where(is_a, 0.0, qn)], axis=0).astype(BF16)
        carry_ref[...] = jnp.zeros_like(carry_ref)
        acc_ref[...] = jnp.zeros_like(acc_ref)
        tile(qq, qi, True)

        def cond(state):
            j, go = state
            return jnp.logical_and(j >= 0, go > 0)

        def body(state):
            j, _ = state
            tile(qq, j, False)
            go = (jnp.max(carry_ref[...]) > ATT_SKIP_LOG).astype(jnp.int32)
            return j - 1, go

        lax.while_loop(cond, body, (qi - 1, jnp.int32(1)))
        o_ref[0, rows, :] = acc_ref[...].astype(BF16)
        return 0

    lax.fori_loop(0, nblk, qblock, 0)


def _sb_attn(qkv, q_gain2, k_gain2):
    bsz, seq, _ = qkv.shape
    ngroups = D_MODEL // LANES
    blk = ATT_BLOCK
    kern = functools.partial(_sb_attn_kernel, seq=seq)
    return pl.pallas_call(
        kern,
        grid=(bsz, ngroups),
        in_specs=[
            pl.BlockSpec((1, seq, LANES), lambda b, g: (b, 0, g)),
            pl.BlockSpec((1, seq, LANES), lambda b, g: (b, 0, ngroups + g)),
            pl.BlockSpec((1, seq, LANES), lambda b, g: (b, 0, 2 * ngroups + g)),
            _const_spec((1, LANES)),
            _const_spec((1, LANES)),
        ],
        out_specs=pl.BlockSpec((1, seq, LANES), lambda b, g: (b, 0, g)),
        out_shape=jax.ShapeDtypeStruct((bsz, seq, D_MODEL), BF16),
        scratch_shapes=[
            pltpu.VMEM((seq, LANES), BF16),
            pltpu.VMEM((seq // blk, 2 * blk, LANES), BF16),
            pltpu.VMEM((2 * blk, 2 * blk), BF16),
            pltpu.VMEM((2 * blk, blk), F32),
            pltpu.VMEM((blk, LANES), F32),
        ],
        compiler_params=pltpu.CompilerParams(
            dimension_semantics=("arbitrary", "arbitrary"), vmem_limit_bytes=VMEM_LIMIT_BYTES),
        name="sb_attn",
    )(qkv, qkv, qkv, q_gain2, k_gain2)


def _hg_proj_kernel(x_ref, g_ref, w_ref, lb_logits_ref, q_ref, lf_ref, i_ref, gs_ref, *, layer_j):
    h = _rmsnorm(x_ref[...], g_ref[...]).astype(BF16)

    logits = lb_logits_ref[...].astype(F32)
    p = jnp.exp(logits - jnp.max(logits, axis=0, keepdims=True))
    p = p / jnp.sum(p, axis=0, keepdims=True)
    lb = jnp.sum(p[0:layer_j + 1], axis=0, keepdims=True) - p[0:1]

    def proj(c):
        return _dot(h, w_ref[:, c * D_MODEL:(c + 1) * D_MODEL])

    q = proj(0)
    q_ref[...] = (q * jax.nn.sigmoid(q)).astype(BF16)
    forget = lb + (1.0 - lb) * jax.nn.sigmoid(proj(1))
    lf_ref[...] = jnp.log(forget)
    i_ref[...] = proj(2).astype(BF16)
    gs_ref[...] = jax.nn.sigmoid(proj(3)).astype(BF16)


def _hg_proj(x, gain, w, lb_logits, layer_j):
    n = x.shape[0]
    tile_spec = pl.BlockSpec((TOKEN_TILE, D_MODEL), lambda i: (i, 0))
    kern = functools.partial(_hg_proj_kernel, layer_j=layer_j)
    return pl.pallas_call(
        kern,
        grid=(n // TOKEN_TILE,),
        in_specs=[
            tile_spec,
            _const_spec((1, D_MODEL)),
            _const_spec((D_MODEL, 4 * D_MODEL)),
            _const_spec(lb_logits.shape),
        ],
        out_specs=[tile_spec, tile_spec, tile_spec, tile_spec],
        out_shape=[
            jax.ShapeDtypeStruct((n, D_MODEL), BF16),
            jax.ShapeDtypeStruct((n, D_MODEL), F32),
            jax.ShapeDtypeStruct((n, D_MODEL), BF16),
            jax.ShapeDtypeStruct((n, D_MODEL), BF16),
        ],
        compiler_params=pltpu.CompilerParams(
            dimension_semantics=("arbitrary",), vmem_limit_bytes=VMEM_LIMIT_BYTES),
        name="hg_proj",
    )(x, gain, w, lb_logits)


def _hg_scan_kernel(q_ref, lf_ref, v_ref, gs_ref, ng_ref, o_ref,
                    st_ref, b_scr, k_scr, q_scr, v_scr, *, seq):
    ch, sub = HG_CHUNK, HG_SUB
    nsub = ch // sub
    st_ref[...] = jnp.zeros_like(st_ref)

    r = lax.broadcasted_iota(jnp.int32, (ch, 3 * ch), 0)
    c = lax.broadcasted_iota(jnp.int32, (ch, 3 * ch), 1)
    tri3 = ((c % ch) <= r).astype(BF16)

    rr = lax.broadcasted_iota(jnp.int32, (ch, ch), 0) // sub
    cc = lax.broadcasted_iota(jnp.int32, (ch, ch), 1) // sub
    sub_row = lax.broadcasted_iota(jnp.int32, (sub, 1), 0)

    def chunk(ci, _):
        rows = pl.ds(pl.multiple_of(ci * ch, ch), ch)
        lf = lf_ref[0, rows, :]
        b = _dot(tri3, jnp.concatenate(_split3(lf), axis=0))
        kf = 1.0 - jnp.exp(lf)
        qf = q_ref[0, rows, :].astype(F32)
        vb = v_ref[0, rows, :]
        b_scr[...] = b
        k_scr[...] = kf
        q_scr[...] = qf
        v_scr[...] = vb.astype(F32)

        st = st_ref[...]
        out = _dot_nt((qf * jnp.exp(b)).astype(BF16), st.astype(BF16))

        b_last = b[ch - 1:ch, :]
        kd = (kf * jnp.exp(b_last - b)).astype(BF16)
        st_ref[...] = st * jnp.exp(b_last) + _dot_tn(vb, kd)

        khat_parts = []
        for j in range(nsub):
            e_j = b[(j + 1) * sub - 1:(j + 1) * sub, :]
            sl = slice(j * sub, (j + 1) * sub)
            khat_parts.append(kf[sl] * jnp.exp(e_j - b[sl]))
        khat = jnp.concatenate(khat_parts, axis=0).astype(BF16)
        scores = jnp.zeros((ch, ch), F32)
        for j in range(nsub - 1):
            e_j = b[(j + 1) * sub - 1:(j + 1) * sub, :]
            qhat = (qf * jnp.exp(jnp.minimum(b - e_j, 0.0))).astype(BF16)
            scores = scores + jnp.where((cc == j) & (rr > j), _dot_nt(qhat, khat), 0.0)
        out = out + _dot(scores.astype(BF16), vb)

        diag_parts = []
        for j in range(nsub):
            sl = slice(j * sub, (j + 1) * sub)
            bq = b_scr[sl, :]
            qq = q_scr[sl, :]
            acc = jnp.zeros((sub, HG_HEAD_DIM), F32)
            for s in range(sub):
                row = slice(j * sub + s, j * sub + s + 1)
                decay = jnp.exp(jnp.minimum(bq - b_scr[row, :], 0.0))
                a = jnp.sum(qq * k_scr[row, :] * decay, axis=-1, keepdims=True)
                a = jnp.where(sub_row >= s, a, 0.0)
                acc = acc + a * v_scr[row, :]
            diag_parts.append(acc)
        out = out + jnp.concatenate(diag_parts, axis=0)

        y = _rmsnorm(out, ng_ref[...]) * gs_ref[0, rows, :].astype(F32)
        o_ref[0, rows, :] = y.astype(BF16)
        return 0

    lax.fori_loop(0, seq // ch, chunk, 0)


def _hg_scan(q, lf, v, gs, norm_gain):
    bsz, seq, _ = q.shape
    nheads = D_MODEL // HG_HEAD_DIM
    head_spec = pl.BlockSpec((1, seq, HG_HEAD_DIM), lambda b, h: (b, 0, h))
    kern = functools.partial(_hg_scan_kernel, seq=seq)
    return pl.pallas_call(
        kern,
        grid=(bsz, nheads),
        in_specs=[head_spec, head_spec, head_spec, head_spec, _const_spec((1, HG_HEAD_DIM))],
        out_specs=head_spec,
        out_shape=jax.ShapeDtypeStruct((bsz, seq, D_MODEL), BF16),
        scratch_shapes=[
            pltpu.VMEM((HG_HEAD_DIM, HG_HEAD_DIM), F32),
            pltpu.VMEM((HG_CHUNK, HG_HEAD_DIM), F32),
            pltpu.VMEM((HG_CHUNK, HG_HEAD_DIM), F32),
            pltpu.VMEM((HG_CHUNK, HG_HEAD_DIM), F32),
            pltpu.VMEM((HG_CHUNK, HG_HEAD_DIM), F32),
        ],
        compiler_params=pltpu.CompilerParams(
            dimension_semantics=("arbitrary", "arbitrary"), vmem_limit_bytes=VMEM_LIMIT_BYTES),
        name="hg_scan",
    )(q, lf, v, gs, norm_gain)


def _out_mlp_kernel(x_ref, y_ref, wo_ref, g_ref, w1_ref, w2_ref, o_ref):
    x1 = x_ref[...] + _dot(y_ref[...], wo_ref[...])
    h = _rmsnorm(x1, g_ref[...]).astype(BF16)
    acc = x1
    for c in range(D_FF // COL_CHUNK):
        cols = slice(c * COL_CHUNK, (c + 1) * COL_CHUNK)
        a = jnp.maximum(_dot(h, w1_ref[:, cols]), 0.0)
        acc = acc + _dot((a * a).astype(BF16), w2_ref[cols, :])
    o_ref[...] = acc


def _out_mlp(x, y, wo, gain, w1, w2):
    n = x.shape[0]
    tile_spec = pl.BlockSpec((TOKEN_TILE, D_MODEL), lambda i: (i, 0))
    return pl.pallas_call(
        _out_mlp_kernel,
        grid=(n // TOKEN_TILE,),
        in_specs=[
            tile_spec,
            tile_spec,
            _const_spec((D_MODEL, D_MODEL)),
            _const_spec((1, D_MODEL)),
            _const_spec((D_MODEL, D_FF)),
            _const_spec((D_FF, D_MODEL)),
        ],
        out_specs=tile_spec,
        out_shape=jax.ShapeDtypeStruct((n, D_MODEL), F32),
        compiler_params=pltpu.CompilerParams(
            dimension_semantics=("arbitrary",), vmem_limit_bytes=VMEM_LIMIT_BYTES),
        name="out_mlp",
    )(x, y, wo, gain, w1, w2)


def kernel(x, norm_gains, sb_w_qkv, sb_q_gain, sb_k_gain, sb_w_o, hg_w_in, hg_lb_logits, hg_norm_gain, hg_w_o, mlp_w1, mlp_w2):
    bsz, seq, d = x.shape
    depth = norm_gains.shape[0]
    n = bsz * seq
    assert d == D_MODEL and n % TOKEN_TILE == 0 and seq % 512 == 0
    xf = x.reshape(n, d)
    for layer in range(depth):
        j = layer // 2
        g_mix = norm_gains[layer, 0].reshape(1, d)
        g_mlp = norm_gains[layer, 1].reshape(1, d)
        if layer % 2 == 0:
            qkv = _sb_proj(xf, g_mix, sb_w_qkv[j].astype(BF16))
            y = _sb_attn(qkv.reshape(bsz, seq, 3 * d),
                         jnp.tile(sb_q_gain[j], 2).reshape(1, LANES),
                         jnp.tile(sb_k_gain[j], 2).reshape(1, LANES))
            wo = sb_w_o[j]
        else:
            q, lf, iv, gs = _hg_proj(xf, g_mix, hg_w_in[j].astype(BF16), hg_lb_logits, j)
            shp = (bsz, seq, d)
            y = _hg_scan(q.reshape(shp), lf.reshape(shp), iv.reshape(shp), gs.reshape(shp),
                         hg_norm_gain[j].reshape(1, HG_HEAD_DIM))
            wo = hg_w_o[j]
        xf = _out_mlp(xf, y.reshape(n, d), wo.astype(BF16), g_mlp,
                      mlp_w1[layer].astype(BF16), mlp_w2[layer].astype(BF16))
    return xf.reshape(bsz, seq, d)
```

```python
import functools
import math

import jax
import jax.numpy as jnp
from jax import lax
from jax.experimental import pallas as pl
from jax.experimental.pallas import tpu as pltpu

F32 = jnp.float32
BF16 = jnp.bfloat16

D_MODEL = 1024
D_FF = 4 * D_MODEL
EPS = 1e-6
LOG2_E = 1.4426950408889634
SB_HEAD_DIM = 64
HG_HEAD_DIM = 128
LANES = 128
VMEM_LIMIT_BYTES = 56 * 1024 * 1024

TOKEN_TILE = 512
COL_CHUNK = 1024

ATT_BLOCK = 128
ATT_GROUPS = 4
ATT_SKIP_LOG = -90.0
ATT_MASKED_LOGIT = -1e30

HG_CHUNK = 64
HG_SUB = 8
HG_GROUP = 4
HG_SEQ_TILE = 1024


def _const_spec(shape):
    return pl.BlockSpec(shape, lambda *_: (0,) * len(shape), pipeline_mode=pl.Buffered(1))


def _rmsnorm(x, gain):
    ms = jnp.mean(x * x, axis=-1, keepdims=True)
    return x * lax.rsqrt(ms + EPS) * gain


def _split2(x):
    hi = x.astype(BF16)
    lo = (x - hi.astype(F32)).astype(BF16)
    return hi, lo


def _split3(x):
    h1 = x.astype(BF16)
    r1 = x - h1.astype(F32)
    h2 = r1.astype(BF16)
    h3 = (r1 - h2.astype(F32)).astype(BF16)
    return h1, h2, h3


def _dot(a, b):
    return jnp.dot(a, b, preferred_element_type=F32)


def _dot_nt(a, b):
    return lax.dot_general(a, b, (((1,), (1,)), ((), ())), preferred_element_type=F32)


def _dot_tn(a, b):
    return lax.dot_general(a, b, (((0,), (0,)), ((), ())), preferred_element_type=F32)


def _sb_proj_kernel(x_ref, g_ref, w_ref, o_ref):
    h = _rmsnorm(x_ref[...], g_ref[...]).astype(BF16)
    for c in range(o_ref.shape[1] // COL_CHUNK):
        cols = slice(c * COL_CHUNK, (c + 1) * COL_CHUNK)
        o_ref[:, cols] = _dot(h, w_ref[:, cols]).astype(BF16)


def _sb_proj(x, gain, w):
    n = x.shape[0]
    dout = w.shape[1]
    return pl.pallas_call(
        _sb_proj_kernel,
        grid=(n // TOKEN_TILE,),
        in_specs=[
            pl.BlockSpec((TOKEN_TILE, D_MODEL), lambda i: (i, 0)),
            _const_spec((1, D_MODEL)),
            _const_spec((D_MODEL, dout)),
        ],
        out_specs=pl.BlockSpec((TOKEN_TILE, dout), lambda i: (i, 0)),
        out_shape=jax.ShapeDtypeStruct((n, dout), BF16),
        compiler_params=pltpu.CompilerParams(
            dimension_semantics=("arbitrary",), vmem_limit_bytes=VMEM_LIMIT_BYTES),
        name="sb_proj",
    )(x, gain, w)


def _sb_attn_kernel(q_ref, k_ref, v_ref, qg_ref, kg_ref, o_ref,
                    kn_ref, uu_ref, carry_ref, acc_ref, *, seq):
    nblk = seq // ATT_BLOCK
    blk = ATT_BLOCK
    ng = ATT_GROUPS
    lane = lax.broadcasted_iota(jnp.int32, (1, LANES), 1)
    is_a = lane < SB_HEAD_DIM

    r = lax.broadcasted_iota(jnp.int32, (2 * LANES, LANES), 0)
    c = lax.broadcasted_iota(jnp.int32, (2 * LANES, LANES), 1)
    group_ones = (((r & (LANES - 1)) >> 6) == (c >> 6)).astype(BF16)

    def head_norm(xf, gain):
        hi, lo = _split2(xf * xf)
        ss = _dot(jnp.concatenate([hi, lo], axis=1), group_ones)
        return xf * lax.rsqrt(ss * (1.0 / SB_HEAD_DIM) + EPS) * gain

    r = lax.broadcasted_iota(jnp.int32, (2 * blk, 2 * blk), 0) & (blk - 1)
    c = lax.broadcasted_iota(jnp.int32, (2 * blk, 2 * blk), 1)
    uu_ref[...] = -((c >= blk) | (r > c)).astype(BF16)

    def lanes_of(g):
        return slice(g * LANES, (g + 1) * LANES)

    def kprep(i, _):
        rows = pl.ds(pl.multiple_of(i * 512, 512), 512)
        for g in range(ng):
            kn_ref[g, rows, :] = head_norm(k_ref[0, rows, lanes_of(g)].astype(F32), kg_ref[...]).astype(BF16)
        return 0
    lax.fori_loop(0, seq // 512, kprep, 0)

    row2 = lax.broadcasted_iota(jnp.int32, (2 * blk, blk), 0) & (blk - 1)
    col2 = lax.broadcasted_iota(jnp.int32, (2 * blk, blk), 1)
    causal = jnp.concatenate([col2 < row2] * ng, axis=0)

    def stage_logits(qqs, j):
        rows = pl.ds(pl.multiple_of(j * blk, blk), blk)
        return jnp.concatenate([_dot_nt(qqs[g], kn_ref[g, rows, :]) for g in range(ng)], axis=0)

    def stage_cumsum(z, mask):
        if mask is not None:
            z = jnp.where(mask, z, ATT_MASKED_LOGIT)
        neg_stay = jnp.maximum(z, 0.0) + jnp.log2(1.0 + jnp.exp2(-jnp.abs(z)))
        hi, lo = _split2(neg_stay)
        cs = _dot(jnp.concatenate([hi, lo], axis=1), uu_ref[...])
        return z - neg_stay + cs[:, :blk], cs[:, blk:]

    def stage_weights(logw):
        return jnp.exp2(logw).astype(BF16)

    def value_rows(g, j):
        rows = pl.ds(pl.multiple_of(j * blk, blk), blk)
        vb = v_ref[0, rows, lanes_of(g)]
        zero = jnp.zeros_like(vb)
        return jnp.concatenate([jnp.where(is_a, vb, zero), jnp.where(is_a, zero, vb)], axis=0)

    def pair_cols(wb, g):
        base = g * 2 * blk
        return jnp.concatenate([wb[base:base + blk], wb[base + blk:base + 2 * blk]], axis=1)

    scale = LOG2_E / math.sqrt(SB_HEAD_DIM)

    def load_queries(qi):
        rows = pl.ds(pl.multiple_of(qi * blk, blk), blk)
        qqs = []
        for g in range(ng):
            qn = head_norm(q_ref[0, rows, lanes_of(g)].astype(F32), qg_ref[...]) * scale
            qqs.append(jnp.concatenate([jnp.where(is_a, qn, 0.0), jnp.where(is_a, 0.0, qn)],
                                       axis=0).astype(BF16))
        return rows, qqs

    rows0, qqs0 = load_queries(0)
    logw, _ = stage_cumsum(stage_logits(qqs0, 0), causal)
    wb = stage_weights(logw)
    for g in range(ng):
        o_ref[0, rows0, lanes_of(g)] = _dot(pair_cols(wb, g), value_rows(g, 0)).astype(BF16)

    def qblock(qi, _):
        rows, qqs = load_queries(qi)
        logw_d, tot_d = stage_cumsum(stage_logits(qqs, qi), causal)
        logw_o, tot_o = stage_cumsum(stage_logits(qqs, qi - 1), None)
        wb_d = stage_weights(logw_d)
        wb_o = stage_weights(logw_o + tot_d)
        carry_ref[...] = tot_d + tot_o
        for g in range(ng):
            acc_ref[g] = (_dot(pair_cols(wb_d, g), value_rows(g, qi))
                          + _dot(pair_cols(wb_o, g), value_rows(g, qi - 1)))

        def cond(state):
            j, go = state
            return jnp.logical_and(j >= 0, go > 0)

        def body(state):
            j, _ = state
            logw, tot = stage_cumsum(stage_logits(qqs, j), None)
            carry = carry_ref[...]
            wb = stage_weights(logw + carry)
            carry_ref[...] = carry + tot
            for g in range(ng):
                acc_ref[g] += _dot(pair_cols(wb, g), value_rows(g, j))
            go = (jnp.max(carry_ref[...]) > ATT_SKIP_LOG * LOG2_E).astype(jnp.int32)
            return j - 1, go

        go0 = (jnp.max(carry_ref[...]) > ATT_SKIP_LOG * LOG2_E).astype(jnp.int32)
        lax.while_loop(cond, body, (qi - 2, go0))
        for g in range(ng):
            o_ref[0, rows, lanes_of(g)] = acc_ref[g].astype(BF16)
        return 0

    lax.fori_loop(1, nblk, qblock, 0)


def _sb_attn(qkv, q_gain2, k_gain2):
    bsz, seq, _ = qkv.shape
    width = ATT_GROUPS * LANES
    nsteps = D_MODEL // width
    blk = ATT_BLOCK
    kern = functools.partial(_sb_attn_kernel, seq=seq)
    return pl.pallas_call(
        kern,
        grid=(bsz, nsteps),
        in_specs=[
            pl.BlockSpec((1, seq, width), lambda b, g: (b, 0, g)),
            pl.BlockSpec((1, seq, width), lambda b, g: (b, 0, nsteps + g), pipeline_mode=pl.Buffered(1)),
            pl.BlockSpec((1, seq, width), lambda b, g: (b, 0, 2 * nsteps + g), pipeline_mode=pl.Buffered(1)),
            _const_spec((1, LANES)),
            _const_spec((1, LANES)),
        ],
        out_specs=pl.BlockSpec((1, seq, width), lambda b, g: (b, 0, g)),
        out_shape=jax.ShapeDtypeStruct((bsz, seq, D_MODEL), BF16),
        scratch_shapes=[
            pltpu.VMEM((ATT_GROUPS, seq, LANES), BF16),
            pltpu.VMEM((2 * blk, 2 * blk), BF16),
            pltpu.VMEM((ATT_GROUPS * 2 * blk, blk), F32),
            pltpu.VMEM((ATT_GROUPS, blk, LANES), F32),
        ],
        compiler_params=pltpu.CompilerParams(
            dimension_semantics=("arbitrary", "arbitrary"), vmem_limit_bytes=VMEM_LIMIT_BYTES),
        name="sb_attn",
    )(qkv, qkv, qkv, q_gain2, k_gain2)


def _hg_proj_kernel(x_ref, g_ref, w_ref, lb_logits_ref, q_ref, lf_ref, i_ref, gs_ref, *, layer_j):
    h = _rmsnorm(x_ref[...], g_ref[...]).astype(BF16)

    logits = lb_logits_ref[...].astype(F32)
    p = jnp.exp(logits - jnp.max(logits, axis=0, keepdims=True))
    p = p / jnp.sum(p, axis=0, keepdims=True)
    lb = jnp.sum(p[0:layer_j + 1], axis=0, keepdims=True) - p[0:1]

    def proj(c):
        return _dot(h, w_ref[:, c * D_MODEL:(c + 1) * D_MODEL])

    q = proj(0)
    q_ref[...] = (q * jax.nn.sigmoid(q)).astype(BF16)
    forget = lb + (1.0 - lb) * jax.nn.sigmoid(proj(1))
    lf_ref[...] = jnp.log(forget)
    i_ref[...] = proj(2).astype(BF16)
    gs_ref[...] = jax.nn.sigmoid(proj(3)).astype(BF16)


def _hg_proj(x, gain, w, lb_logits, layer_j):
    n = x.shape[0]
    tile_spec = pl.BlockSpec((TOKEN_TILE, D_MODEL), lambda i: (i, 0))
    kern = functools.partial(_hg_proj_kernel, layer_j=layer_j)
    return pl.pallas_call(
        kern,
        grid=(n // TOKEN_TILE,),
        in_specs=[
            tile_spec,
            _const_spec((1, D_MODEL)),
            _const_spec((D_MODEL, 4 * D_MODEL)),
            _const_spec(lb_logits.shape),
        ],
        out_specs=[tile_spec, tile_spec, tile_spec, tile_spec],
        out_shape=[
            jax.ShapeDtypeStruct((n, D_MODEL), BF16),
            jax.ShapeDtypeStruct((n, D_MODEL), F32),
            jax.ShapeDtypeStruct((n, D_MODEL), BF16),
            jax.ShapeDtypeStruct((n, D_MODEL), BF16),
        ],
        compiler_params=pltpu.CompilerParams(
            dimension_semantics=("arbitrary",), vmem_limit_bytes=VMEM_LIMIT_BYTES),
        name="hg_proj",
    )(x, gain, w, lb_logits)


def _hg_scan_kernel(q_ref, lf_ref, v_ref, gs_ref, ng_ref, o_ref,
                    st_ref, b_scr, c_scr, *, seq_tile):
    ch, sub, nh, hd = HG_CHUNK, HG_SUB, HG_GROUP, HG_HEAD_DIM

    @pl.when(pl.program_id(2) == 0)
    def _():
        st_ref[...] = jnp.zeros_like(st_ref)

    r = lax.broadcasted_iota(jnp.int32, (ch, 3 * ch), 0)
    c = lax.broadcasted_iota(jnp.int32, (ch, 3 * ch), 1)
    tri3 = ((c % ch) <= r).astype(BF16)

    levels = []
    size = ch // 2
    while size >= sub:
        levels.append(size)
        size //= 2
    rr = lax.broadcasted_iota(jnp.int32, (ch, ch), 0)
    cc = lax.broadcasted_iota(jnp.int32, (ch, ch), 1)
    level_masks = [((rr // (2 * s)) == (cc // (2 * s))) & (((rr // s) & 1) == 1) & (((cc // s) & 1) == 0)
                   for s in levels]
    sub_row = lax.broadcasted_iota(jnp.int32, (sub, ch), 0)
    sub_col = lax.broadcasted_iota(jnp.int32, (sub, ch), 1)

    def heads(x):
        return [x[:, h * hd:(h + 1) * hd] for h in range(nh)]

    def row_bcast(ref, h, row, nrows):
        return jnp.broadcast_to(ref[h, row:row + 1, :], (nrows, hd))

    def chunk(ci, _):
        rows = pl.ds(pl.multiple_of(ci * ch, ch), ch)
        lf = lf_ref[0, rows, :]
        b = _dot(tri3, jnp.concatenate(_split3(lf), axis=0))
        b2 = heads(b * LOG2_E)
        kf = heads(jnp.maximum(1.0 - jnp.exp(lf), 0.0))
        qf = heads(q_ref[0, rows, :].astype(F32))
        vb = heads(v_ref[0, rows, :])
        gs = heads(gs_ref[0, rows, :].astype(F32))
        for h in range(nh):
            b_scr[h] = b2[h]
            c_scr[h] = b2[h] - jnp.log2(kf[h])

        out = []
        for h in range(nh):
            b2_last = row_bcast(b_scr, h, ch - 1, ch)
            st = st_ref[h]
            out.append(_dot_nt((qf[h] * jnp.exp2(b2[h])).astype(BF16), st.astype(BF16)))
            kd = (kf[h] * jnp.exp2(b2_last - b2[h])).astype(BF16)
            st_ref[h] = st * jnp.exp2(b2_last[0:1, :]) + _dot_tn(vb[h], kd)

        blocks = [[] for _ in range(nh)]
        for i in range(ch // sub):
            sl = slice(i * sub, (i + 1) * sub)
            a_blk = [jnp.zeros((sub, ch), F32) for _ in range(nh)]
            for s in range(sub):
                put = (sub_col == i * sub + s) & (sub_row >= s)
                for h in range(nh):
                    crow = row_bcast(c_scr, h, i * sub + s, sub)
                    a = jnp.sum(qf[h][sl] * jnp.exp2(b2[h][sl] - crow), axis=-1, keepdims=True)
                    a_blk[h] = jnp.where(put, a, a_blk[h])
            for h in range(nh):
                blocks[h].append(a_blk[h])
        scores = [jnp.concatenate(blocks[h], axis=0) for h in range(nh)]

        for s, mask in zip(levels, level_masks):
            for h in range(nh):
                anchor = jnp.concatenate([row_bcast(b_scr, h, p * 2 * s + s - 1, 2 * s)
                                          for p in range(ch // (2 * s))], axis=0)
                qk = jnp.concatenate([(qf[h] if (i & 1) else kf[h])[i * s:(i + 1) * s]
                                      for i in range(ch // s)], axis=0)
                x = (qk * jnp.exp2(-jnp.abs(b2[h] - anchor))).astype(BF16)
                scores[h] = jnp.where(mask, _dot_nt(x, x), scores[h])

        for h in range(nh):
            o_h = out[h] + _dot(scores[h].astype(BF16), vb[h])
            y = _rmsnorm(o_h, ng_ref[...]) * gs[h]
            o_ref[0, rows, h * hd:(h + 1) * hd] = y.astype(BF16)
        return 0

    lax.fori_loop(0, seq_tile // ch, chunk, 0, unroll=2)


def _hg_scan(q, lf, v, gs, norm_gain):
    bsz, seq, _ = q.shape
    width = HG_GROUP * HG_HEAD_DIM
    seq_tile = min(HG_SEQ_TILE, seq)
    blk_spec = pl.BlockSpec((1, seq_tile, width), lambda b, h, t: (b, t, h))
    kern = functools.partial(_hg_scan_kernel, seq_tile=seq_tile)
    chunk_scratch = pltpu.VMEM((HG_GROUP, HG_CHUNK, HG_HEAD_DIM), F32)
    return pl.pallas_call(
        kern,
        grid=(bsz, D_MODEL // width, seq // seq_tile),
        in_specs=[blk_spec, blk_spec, blk_spec, blk_spec, _const_spec((1, HG_HEAD_DIM))],
        out_specs=blk_spec,
        out_shape=jax.ShapeDtypeStruct((bsz, seq, D_MODEL), BF16),
        scratch_shapes=[
            pltpu.VMEM((HG_GROUP, HG_HEAD_DIM, HG_HEAD_DIM), F32),
            chunk_scratch, chunk_scratch,
        ],
        compiler_params=pltpu.CompilerParams(
            dimension_semantics=("arbitrary", "arbitrary", "arbitrary"), vmem_limit_bytes=VMEM_LIMIT_BYTES),
        name="hg_scan",
    )(q, lf, v, gs, norm_gain)


def _out_mlp_kernel(x_ref, y_ref, wo_ref, g_ref, w1_ref, w2_ref, o_ref):
    x1 = x_ref[...] + _dot(y_ref[...], wo_ref[...])
    h = _rmsnorm(x1, g_ref[...]).astype(BF16)
    acc = x1
    for c in range(D_FF // COL_CHUNK):
        cols = slice(c * COL_CHUNK, (c + 1) * COL_CHUNK)
        a = jnp.maximum(_dot(h, w1_ref[:, cols]), 0.0)
        acc = acc + _dot((a * a).astype(BF16), w2_ref[cols, :])
    o_ref[...] = acc


def _out_mlp(x, y, wo, gain, w1, w2):
    n = x.shape[0]
    tile_spec = pl.BlockSpec((TOKEN_TILE, D_MODEL), lambda i: (i, 0))
    return pl.pallas_call(
        _out_mlp_kernel,
        grid=(n // TOKEN_TILE,),
        in_specs=[
            tile_spec,
            tile_spec,
            _const_spec((D_MODEL, D_MODEL)),
            _const_spec((1, D_MODEL)),
            _const_spec((D_MODEL, D_FF)),
            _const_spec((D_FF, D_MODEL)),
        ],
        out_specs=tile_spec,
        out_shape=jax.ShapeDtypeStruct((n, D_MODEL), F32),
        compiler_params=pltpu.CompilerParams(
            dimension_semantics=("arbitrary",), vmem_limit_bytes=VMEM_LIMIT_BYTES),
        name="out_mlp",
    )(x, y, wo, gain, w1, w2)


def kernel(x, norm_gains, sb_w_qkv, sb_q_gain, sb_k_gain, sb_w_o, hg_w_in, hg_lb_logits, hg_norm_gain, hg_w_o, mlp_w1, mlp_w2):
    bsz, seq, d = x.shape
    depth = norm_gains.shape[0]
    n = bsz * seq
    assert d == D_MODEL and n % TOKEN_TILE == 0 and seq % 512 == 0
    xf = x.reshape(n, d)
    for layer in range(depth):
        j = layer // 2
        g_mix = norm_gains[layer, 0].reshape(1, d)
        g_mlp = norm_gains[layer, 1].reshape(1, d)
        if layer % 2 == 0:
            qkv = _sb_proj(xf, g_mix, sb_w_qkv[j].astype(BF16))
            y = _sb_attn(qkv.reshape(bsz, seq, 3 * d),
                         jnp.tile(sb_q_gain[j], 2).reshape(1, LANES),
                         jnp.tile(sb_k_gain[j], 2).reshape(1, LANES))
            wo = sb_w_o[j]
        else:
            q, lf, iv, gs = _hg_proj(xf, g_mix, hg_w_in[j].astype(BF16), hg_lb_logits, j)
            shp = (bsz, seq, d)
            y = _hg_scan(q.reshape(shp), lf.reshape(shp), iv.reshape(shp), gs.reshape(shp),
                         hg_norm_gain[j].reshape(1, HG_HEAD_DIM))
            wo = hg_w_o[j]
        xf = _out_mlp(xf, y.reshape(n, d), wo.astype(BF16), g_mlp,
                      mlp_w1[layer].astype(BF16), mlp_w2[layer].astype(BF16))
    return xf.reshape(bsz, seq, d)
```

```python
import functools
import math

import jax
import jax.numpy as jnp
from jax import lax
from jax.experimental import pallas as pl
from jax.experimental.pallas import tpu as pltpu

F32 = jnp.float32
BF16 = jnp.bfloat16

D_MODEL = 1024
D_FF = 4 * D_MODEL
EPS = 1e-6
LOG2_E = 1.4426950408889634
SB_HEAD_DIM = 64
HG_HEAD_DIM = 128
LANES = 128
VMEM_LIMIT_BYTES = 56 * 1024 * 1024

TOKEN_TILE = 512
COL_CHUNK = 1024

ATT_BLOCK = 128
ATT_GROUPS = 4
ATT_SKIP_LOG = -90.0
ATT_MASKED_LOGIT = -1e30
ATT_TOP_ROWS = 64
ATT_SWEEP = 2

HG_CHUNK = 64
HG_SUB = 8
HG_GROUP = 4
HG_SEQ_TILE = 1024


def _const_spec(shape):
    return pl.BlockSpec(shape, lambda *_: (0,) * len(shape), pipeline_mode=pl.Buffered(1))


def _rmsnorm(x, gain):
    ms = jnp.mean(x * x, axis=-1, keepdims=True)
    return x * lax.rsqrt(ms + EPS) * gain


def _split2(x):
    hi = x.astype(BF16)
    lo = (x - hi.astype(F32)).astype(BF16)
    return hi, lo


def _split3(x):
    h1 = x.astype(BF16)
    r1 = x - h1.astype(F32)
    h2 = r1.astype(BF16)
    h3 = (r1 - h2.astype(F32)).astype(BF16)
    return h1, h2, h3


def _dot(a, b):
    return jnp.dot(a, b, preferred_element_type=F32)


def _dot_nt(a, b):
    return lax.dot_general(a, b, (((1,), (1,)), ((), ())), preferred_element_type=F32)


def _dot_tn(a, b):
    return lax.dot_general(a, b, (((0,), (0,)), ((), ())), preferred_element_type=F32)


def _sb_proj_kernel(x_ref, g_ref, w_ref, o_ref):
    h = _rmsnorm(x_ref[...], g_ref[...]).astype(BF16)
    for c in range(o_ref.shape[1] // COL_CHUNK):
        cols = slice(c * COL_CHUNK, (c + 1) * COL_CHUNK)
        o_ref[:, cols] = _dot(h, w_ref[:, cols]).astype(BF16)


def _sb_proj(x, gain, w):
    n = x.shape[0]
    dout = w.shape[1]
    return pl.pallas_call(
        _sb_proj_kernel,
        grid=(n // TOKEN_TILE,),
        in_specs=[
            pl.BlockSpec((TOKEN_TILE, D_MODEL), lambda i: (i, 0)),
            _const_spec((1, D_MODEL)),
            _const_spec((D_MODEL, dout)),
        ],
        out_specs=pl.BlockSpec((TOKEN_TILE, dout), lambda i: (i, 0)),
        out_shape=jax.ShapeDtypeStruct((n, dout), BF16),
        compiler_params=pltpu.CompilerParams(
            dimension_semantics=("arbitrary",), vmem_limit_bytes=VMEM_LIMIT_BYTES),
        name="sb_proj",
    )(x, gain, w)


def _sb_attn_kernel(q_ref, k_ref, v_ref, qg_ref, kg_ref, o_ref,
                    kn_ref, uu_ref, carry_ref, acc_ref, *, seq):
    nblk = seq // ATT_BLOCK
    blk = ATT_BLOCK
    ng = ATT_GROUPS
    lane = lax.broadcasted_iota(jnp.int32, (1, LANES), 1)
    is_a = lane < SB_HEAD_DIM

    r = lax.broadcasted_iota(jnp.int32, (2 * LANES, LANES), 0)
    c = lax.broadcasted_iota(jnp.int32, (2 * LANES, LANES), 1)
    group_ones = (((r & (LANES - 1)) >> 6) == (c >> 6)).astype(BF16)

    def head_norm(xf, gain):
        hi, lo = _split2(xf * xf)
        ss = _dot(jnp.concatenate([hi, lo], axis=1), group_ones)
        return xf * lax.rsqrt(ss * (1.0 / SB_HEAD_DIM) + EPS) * gain

    r = lax.broadcasted_iota(jnp.int32, (2 * blk, 2 * blk), 0) & (blk - 1)
    c = lax.broadcasted_iota(jnp.int32, (2 * blk, 2 * blk), 1)
    uu_ref[...] = -((c >= blk) | (r > c)).astype(BF16)

    def lanes_of(g):
        return slice(g * LANES, (g + 1) * LANES)

    def kprep(i, _):
        rows = pl.ds(pl.multiple_of(i * 512, 512), 512)
        for g in range(ng):
            kn_ref[g, rows, :] = head_norm(k_ref[0, rows, lanes_of(g)].astype(F32), kg_ref[...]).astype(BF16)
        return 0
    lax.fori_loop(0, seq // 512, kprep, 0)

    row2 = lax.broadcasted_iota(jnp.int32, (2 * blk, blk), 0) & (blk - 1)
    col2 = lax.broadcasted_iota(jnp.int32, (2 * blk, blk), 1)
    causal = jnp.concatenate([col2 < row2] * ng, axis=0)

    top = ATT_TOP_ROWS
    top_rows = jnp.concatenate([row2 < top] * ng, axis=0)

    def stage_logits(qqs, j, nrows=blk):
        rows = pl.ds(pl.multiple_of(j * blk, blk), blk)
        parts = []
        for g in range(ng):
            qq = qqs[g] if nrows == blk else jnp.concatenate([qqs[g][0:nrows], qqs[g][blk:blk + nrows]], axis=0)
            parts.append(_dot_nt(qq, kn_ref[g, rows, :]))
        return jnp.concatenate(parts, axis=0)

    def stage_cumsum(z):
        neg_stay = jnp.maximum(z, 0.0) + jnp.log2(1.0 + jnp.exp2(-jnp.abs(z)))
        hi, lo = _split2(neg_stay)
        cs = _dot(jnp.concatenate([hi, lo], axis=1), uu_ref[...])
        return z - neg_stay + cs[:, :blk], cs[:, blk:]

    def stage_weights(logw):
        return jnp.exp2(logw).astype(BF16)

    def value_rows(g, j):
        rows = pl.ds(pl.multiple_of(j * blk, blk), blk)
        vb = v_ref[0, rows, lanes_of(g)]
        zero = jnp.zeros_like(vb)
        return jnp.concatenate([jnp.where(is_a, vb, zero), jnp.where(is_a, zero, vb)], axis=0)

    def pair_cols(wb, g, nrows=blk):
        base = g * 2 * nrows
        return jnp.concatenate([wb[base:base + nrows], wb[base + nrows:base + 2 * nrows]], axis=1)

    def head_rows(x, nrows):
        return jnp.concatenate([x[b0:b0 + nrows] for b0 in range(0, ng * 2 * blk, blk)], axis=0)

    scale = LOG2_E / math.sqrt(SB_HEAD_DIM)

    def load_queries(qi):
        rows = pl.ds(pl.multiple_of(qi * blk, blk), blk)
        qqs = []
        for g in range(ng):
            qn = head_norm(q_ref[0, rows, lanes_of(g)].astype(F32), qg_ref[...]) * scale
            qqs.append(jnp.concatenate([jnp.where(is_a, qn, 0.0), jnp.where(is_a, 0.0, qn)],
                                       axis=0).astype(BF16))
        return rows, qqs

    rows0, qqs0 = load_queries(0)
    logw, _ = stage_cumsum(jnp.where(causal, stage_logits(qqs0, 0), ATT_MASKED_LOGIT))
    wb = stage_weights(logw)
    for g in range(ng):
        o_ref[0, rows0, lanes_of(g)] = _dot(pair_cols(wb, g), value_rows(g, 0)).astype(BF16)

    def sweep(qis):
        nq = len(qis)
        loaded = [load_queries(qi) for qi in qis]
        lw_d, tot_d, lw_o, tot_o, lw_t, tot_t = [], [], [], [], [], []
        for k, qi in enumerate(qis):
            qqs = loaded[k][1]
            a, b = stage_cumsum(jnp.where(causal, stage_logits(qqs, qi), ATT_MASKED_LOGIT))
            lw_d.append(a)
            tot_d.append(b)
            a, b = stage_cumsum(stage_logits(qqs, qi - 1))
            lw_o.append(a)
            tot_o.append(b)
            z_t = jnp.where(qi >= 2, stage_logits(qqs, jnp.maximum(qi - 2, 0), top), ATT_MASKED_LOGIT)
            a, b = stage_cumsum(z_t)
            lw_t.append(a)
            tot_t.append(b)
        for k, qi in enumerate(qis):
            carry = tot_d[k] + tot_o[k]
            carry_t = head_rows(carry, top)
            wb_d = stage_weights(lw_d[k])
            wb_o = stage_weights(lw_o[k] + tot_d[k])
            wb_t = stage_weights(lw_t[k] + carry_t)
            carry_ref[k] = carry
            carry_t = carry_t + tot_t[k]
            for i, b0 in enumerate(range(0, ng * 2 * blk, blk)):
                carry_ref[k, b0:b0 + top, :] = carry_t[i * top:(i + 1) * top]
            j2 = jnp.maximum(qi - 2, 0)
            for g in range(ng):
                acc_ref[k, g] = (_dot(pair_cols(wb_d, g), value_rows(g, qi))
                                 + _dot(pair_cols(wb_o, g), value_rows(g, qi - 1)))
                acc_ref[k, g, 0:top, :] += _dot(pair_cols(wb_t, g, top), value_rows(g, j2))

        def live():
            return (jnp.max(carry_ref[...]) > ATT_SKIP_LOG * LOG2_E).astype(jnp.int32)

        def cond(state):
            d, go = state
            return jnp.logical_and(qis[-1] - 2 - d >= 0, go > 0)

        def body(state):
            d, _ = state
            for k, qi in enumerate(qis):
                j = qi - 2 - d
                z = stage_logits(loaded[k][1], jnp.maximum(j, 0))
                dead = jnp.logical_or(j < 0, jnp.logical_and(top_rows, d == 0))
                logw, tot = stage_cumsum(jnp.where(dead, ATT_MASKED_LOGIT, z))
                carry = carry_ref[k]
                wb = stage_weights(logw + carry)
                carry_ref[k] = carry + tot
                for g in range(ng):
                    acc_ref[k, g] += _dot(pair_cols(wb, g), value_rows(g, jnp.maximum(j, 0)))
            return d + 1, live()

        lax.while_loop(cond, body, (jnp.int32(0), live()))
        for k in range(nq):
            for g in range(ng):
                o_ref[0, loaded[k][0], lanes_of(g)] = acc_ref[k, g].astype(BF16)

    nfull = (nblk - 1) // ATT_SWEEP

    def sweep_step(p, _):
        sweep([1 + p * ATT_SWEEP + k for k in range(ATT_SWEEP)])
        return 0

    lax.fori_loop(0, nfull, sweep_step, 0)
    for qi in range(1 + nfull * ATT_SWEEP, nblk):
        sweep([jnp.int32(qi)])


def _sb_attn(qkv, q_gain2, k_gain2):
    bsz, seq, _ = qkv.shape
    width = ATT_GROUPS * LANES
    nsteps = D_MODEL // width
    blk = ATT_BLOCK
    kern = functools.partial(_sb_attn_kernel, seq=seq)
    return pl.pallas_call(
        kern,
        grid=(bsz, nsteps),
        in_specs=[
            pl.BlockSpec((1, seq, width), lambda b, g: (b, 0, g)),
            pl.BlockSpec((1, seq, width), lambda b, g: (b, 0, nsteps + g), pipeline_mode=pl.Buffered(1)),
            pl.BlockSpec((1, seq, width), lambda b, g: (b, 0, 2 * nsteps + g), pipeline_mode=pl.Buffered(1)),
            _const_spec((1, LANES)),
            _const_spec((1, LANES)),
        ],
        out_specs=pl.BlockSpec((1, seq, width), lambda b, g: (b, 0, g)),
        out_shape=jax.ShapeDtypeStruct((bsz, seq, D_MODEL), BF16),
        scratch_shapes=[
            pltpu.VMEM((ATT_GROUPS, seq, LANES), BF16),
            pltpu.VMEM((2 * blk, 2 * blk), BF16),
            pltpu.VMEM((ATT_SWEEP, ATT_GROUPS * 2 * blk, blk), F32),
            pltpu.VMEM((ATT_SWEEP, ATT_GROUPS, blk, LANES), F32),
        ],
        compiler_params=pltpu.CompilerParams(
            dimension_semantics=("arbitrary", "arbitrary"), vmem_limit_bytes=VMEM_LIMIT_BYTES),
        name="sb_attn",
    )(qkv, qkv, qkv, q_gain2, k_gain2)


def _hg_proj_kernel(x_ref, g_ref, w_ref, lb_logits_ref, q_ref, lf_ref, i_ref, gs_ref, *, layer_j):
    h = _rmsnorm(x_ref[...], g_ref[...]).astype(BF16)

    logits = lb_logits_ref[...].astype(F32)
    p = jnp.exp(logits - jnp.max(logits, axis=0, keepdims=True))
    p = p / jnp.sum(p, axis=0, keepdims=True)
    lb = jnp.sum(p[0:layer_j + 1], axis=0, keepdims=True) - p[0:1]

    def proj(c):
        return _dot(h, w_ref[:, c * D_MODEL:(c + 1) * D_MODEL])

    q = proj(0)
    q_ref[...] = (q * jax.nn.sigmoid(q)).astype(BF16)
    forget = lb + (1.0 - lb) * jax.nn.sigmoid(proj(1))
    lf_ref[...] = jnp.log(forget)
    i_ref[...] = proj(2).astype(BF16)
    gs_ref[...] = jax.nn.sigmoid(proj(3)).astype(BF16)


def _hg_proj(x, gain, w, lb_logits, layer_j):
    n = x.shape[0]
    tile_spec = pl.BlockSpec((TOKEN_TILE, D_MODEL), lambda i: (i, 0))
    kern = functools.partial(_hg_proj_kernel, layer_j=layer_j)
    return pl.pallas_call(
        kern,
        grid=(n // TOKEN_TILE,),
        in_specs=[
            tile_spec,
            _const_spec((1, D_MODEL)),
            _const_spec((D_MODEL, 4 * D_MODEL)),
            _const_spec(lb_logits.shape),
        ],
        out_specs=[tile_spec, tile_spec, tile_spec, tile_spec],
        out_shape=[
            jax.ShapeDtypeStruct((n, D_MODEL), BF16),
            jax.ShapeDtypeStruct((n, D_MODEL), F32),
            jax.ShapeDtypeStruct((n, D_MODEL), BF16),
            jax.ShapeDtypeStruct((n, D_MODEL), BF16),
        ],
        compiler_params=pltpu.CompilerParams(
            dimension_semantics=("arbitrary",), vmem_limit_bytes=VMEM_LIMIT_BYTES),
        name="hg_proj",
    )(x, gain, w, lb_logits)


def _hg_scan_kernel(q_ref, lf_ref, v_ref, gs_ref, ng_ref, o_ref,
                    st_ref, b_scr, c_scr, *, seq_tile):
    ch, sub, nh, hd = HG_CHUNK, HG_SUB, HG_GROUP, HG_HEAD_DIM

    @pl.when(pl.program_id(2) == 0)
    def _():
        st_ref[...] = jnp.zeros_like(st_ref)

    r = lax.broadcasted_iota(jnp.int32, (ch, 3 * ch), 0)
    c = lax.broadcasted_iota(jnp.int32, (ch, 3 * ch), 1)
    tri3 = ((c % ch) <= r).astype(BF16)

    levels = []
    size = ch // 2
    while size >= sub:
        levels.append(size)
        size //= 2
    rr = lax.broadcasted_iota(jnp.int32, (ch, ch), 0)
    cc = lax.broadcasted_iota(jnp.int32, (ch, ch), 1)
    level_masks = [((rr // (2 * s)) == (cc // (2 * s))) & (((rr // s) & 1) == 1) & (((cc // s) & 1) == 0)
                   for s in levels]
    sub_row = lax.broadcasted_iota(jnp.int32, (sub, ch), 0)
    sub_col = lax.broadcasted_iota(jnp.int32, (sub, ch), 1)

    def heads(x):
        return [x[:, h * hd:(h + 1) * hd] for h in range(nh)]

    def row_bcast(ref, h, row, nrows):
        return jnp.broadcast_to(ref[h, row:row + 1, :], (nrows, hd))

    def chunk(ci, _):
        rows = pl.ds(pl.multiple_of(ci * ch, ch), ch)
        lf = lf_ref[0, rows, :]
        b = _dot(tri3, jnp.concatenate(_split3(lf), axis=0))
        b2 = heads(b * LOG2_E)
        kf = heads(jnp.maximum(1.0 - jnp.exp(lf), 0.0))
        qf = heads(q_ref[0, rows, :].astype(F32))
        vb = heads(v_ref[0, rows, :])
        gs = heads(gs_ref[0, rows, :].astype(F32))
        for h in range(nh):
            b_scr[h] = b2[h]
            c_scr[h] = b2[h] - jnp.log2(kf[h])

        out = []
        for h in range(nh):
            b2_last = row_bcast(b_scr, h, ch - 1, ch)
            st = st_ref[h]
            out.append(_dot_nt((qf[h] * jnp.exp2(b2[h])).astype(BF16), st.astype(BF16)))
            kd = (kf[h] * jnp.exp2(b2_last - b2[h])).astype(BF16)
            st_ref[h] = st * jnp.exp2(b2_last[0:1, :]) + _dot_tn(vb[h], kd)

        blocks = [[] for _ in range(nh)]
        for i in range(ch // sub):
            sl = slice(i * sub, (i + 1) * sub)
            a_blk = [jnp.zeros((sub, ch), F32) for _ in range(nh)]
            for s in range(sub):
                put = (sub_col == i * sub + s) & (sub_row >= s)
                for h in range(nh):
                    crow = row_bcast(c_scr, h, i * sub + s, sub)
                    a = jnp.sum(qf[h][sl] * jnp.exp2(b2[h][sl] - crow), axis=-1, keepdims=True)
                    a_blk[h] = jnp.where(put, a, a_blk[h])
            for h in range(nh):
                blocks[h].append(a_blk[h])
        scores = [jnp.concatenate(blocks[h], axis=0) for h in range(nh)]

        for s, mask in zip(levels, level_masks):
            for h in range(nh):
                anchor = jnp.concatenate([row_bcast(b_scr, h, p * 2 * s + s - 1, 2 * s)
                                          for p in range(ch // (2 * s))], axis=0)
                qk = jnp.concatenate([(qf[h] if (i & 1) else kf[h])[i * s:(i + 1) * s]
                                      for i in range(ch // s)], axis=0)
                x = (qk * jnp.exp2(-jnp.abs(b2[h] - anchor))).astype(BF16)
                scores[h] = jnp.where(mask, _dot_nt(x, x), scores[h])

        for h in range(nh):
            o_h = out[h] + _dot(scores[h].astype(BF16), vb[h])
            y = _rmsnorm(o_h, ng_ref[...]) * gs[h]
            o_ref[0, rows, h * hd:(h + 1) * hd] = y.astype(BF16)
        return 0

    lax.fori_loop(0, seq_tile // ch, chunk, 0, unroll=4)


def _hg_scan(q, lf, v, gs, norm_gain):
    bsz, seq, _ = q.shape
    width = HG_GROUP * HG_HEAD_DIM
    seq_tile = min(HG_SEQ_TILE, seq)
    blk_spec = pl.BlockSpec((1, seq_tile, width), lambda b, h, t: (b, t, h))
    kern = functools.partial(_hg_scan_kernel, seq_tile=seq_tile)
    chunk_scratch = pltpu.VMEM((HG_GROUP, HG_CHUNK, HG_HEAD_DIM), F32)
    return pl.pallas_call(
        kern,
        grid=(bsz, D_MODEL // width, seq // seq_tile),
        in_specs=[blk_spec, blk_spec, blk_spec, blk_spec, _const_spec((1, HG_HEAD_DIM))],
        out_specs=blk_spec,
        out_shape=jax.ShapeDtypeStruct((bsz, seq, D_MODEL), BF16),
        scratch_shapes=[
            pltpu.VMEM((HG_GROUP, HG_HEAD_DIM, HG_HEAD_DIM), F32),
            chunk_scratch, chunk_scratch,
        ],
        compiler_params=pltpu.CompilerParams(
            dimension_semantics=("arbitrary", "arbitrary", "arbitrary"), vmem_limit_bytes=VMEM_LIMIT_BYTES),
        name="hg_scan",
    )(q, lf, v, gs, norm_gain)


def _out_mlp_kernel(x_ref, y_ref, wo_ref, g_ref, w1_ref, w2_ref, o_ref):
    x1 = x_ref[...] + _dot(y_ref[...], wo_ref[...])
    h = _rmsnorm(x1, g_ref[...]).astype(BF16)
    acc = x1
    for c in range(D_FF // COL_CHUNK):
        cols = slice(c * COL_CHUNK, (c + 1) * COL_CHUNK)
        a = jnp.maximum(_dot(h, w1_ref[:, cols]), 0.0)
        acc = acc + _dot((a * a).astype(BF16), w2_ref[cols, :])
    o_ref[...] = acc


def _out_mlp(x, y, wo, gain, w1, w2):
    n = x.shape[0]
    tile_spec = pl.BlockSpec((TOKEN_TILE, D_MODEL), lambda i: (i, 0))
    return pl.pallas_call(
        _out_mlp_kernel,
        grid=(n // TOKEN_TILE,),
        in_specs=[
            tile_spec,
            tile_spec,
            _const_spec((D_MODEL, D_MODEL)),
            _const_spec((1, D_MODEL)),
            _const_spec((D_MODEL, D_FF)),
            _const_spec((D_FF, D_MODEL)),
        ],
        out_specs=tile_spec,
        out_shape=jax.ShapeDtypeStruct((n, D_MODEL), F32),
        compiler_params=pltpu.CompilerParams(
            dimension_semantics=("arbitrary",), vmem_limit_bytes=VMEM_LIMIT_BYTES),
        name="out_mlp",
    )(x, y, wo, gain, w1, w2)


def kernel(x, norm_gains, sb_w_qkv, sb_q_gain, sb_k_gain, sb_w_o, hg_w_in, hg_lb_logits, hg_norm_gain, hg_w_o, mlp_w1, mlp_w2):
    bsz, seq, d = x.shape
    depth = norm_gains.shape[0]
    n = bsz * seq
    assert d == D_MODEL and n % TOKEN_TILE == 0 and seq % 512 == 0
    xf = x.reshape(n, d)
    for layer in range(depth):
        j = layer // 2
        g_mix = norm_gains[layer, 0].reshape(1, d)
        g_mlp = norm_gains[layer, 1].reshape(1, d)
        if layer % 2 == 0:
            qkv = _sb_proj(xf, g_mix, sb_w_qkv[j].astype(BF16))
            y = _sb_attn(qkv.reshape(bsz, seq, 3 * d),
                         jnp.tile(sb_q_gain[j], 2).reshape(1, LANES),
                         jnp.tile(sb_k_gain[j], 2).reshape(1, LANES))
            wo = sb_w_o[j]
        else:
            q, lf, iv, gs = _hg_proj(xf, g_mix, hg_w_in[j].astype(BF16), hg_lb_logits, j)
            shp = (bsz, seq, d)
            y = _hg_scan(q.reshape(shp), lf.reshape(shp), iv.reshape(shp), gs.reshape(shp),
                         hg_norm_gain[j].reshape(1, HG_HEAD_DIM))
            wo = hg_w_o[j]
        xf = _out_mlp(xf, y.reshape(n, d), wo.astype(BF16), g_mlp,
                      mlp_w1[layer].astype(BF16), mlp_w2[layer].astype(BF16))
    return xf.reshape(bsz, seq, d)
```

```python
import functools
import math

import jax
import jax.numpy as jnp
from jax import lax
from jax.experimental import pallas as pl
from jax.experimental.pallas import tpu as pltpu

F32 = jnp.float32
BF16 = jnp.bfloat16

D_MODEL = 1024
D_FF = 4 * D_MODEL
EPS = 1e-6
LOG2_E = 1.4426950408889634
SB_HEAD_DIM = 64
HG_HEAD_DIM = 128
LANES = 128
VMEM_LIMIT_BYTES = 56 * 1024 * 1024

TOKEN_TILE = 512
COL_CHUNK = 1024

ATT_BLOCK = 128
ATT_GROUPS = 4
ATT_SKIP_LOG = -90.0
ATT_MASKED_LOGIT = -1e30
ATT_TOP_ROWS = 64
ATT_SWEEP = 2

HG_CHUNK = 64
HG_SUB = 8
HG_GROUP = 8
HG_SEQ_TILE = 1024


def _const_spec(shape):
    return pl.BlockSpec(shape, lambda *_: (0,) * len(shape), pipeline_mode=pl.Buffered(1))


def _rmsnorm(x, gain):
    ms = jnp.mean(x * x, axis=-1, keepdims=True)
    return x * lax.rsqrt(ms + EPS) * gain


def _split2(x):
    hi = x.astype(BF16)
    lo = (x - hi.astype(F32)).astype(BF16)
    return hi, lo


def _split3(x):
    h1 = x.astype(BF16)
    r1 = x - h1.astype(F32)
    h2 = r1.astype(BF16)
    h3 = (r1 - h2.astype(F32)).astype(BF16)
    return h1, h2, h3


def _dot(a, b):
    return jnp.dot(a, b, preferred_element_type=F32)


def _dot_nt(a, b):
    return lax.dot_general(a, b, (((1,), (1,)), ((), ())), preferred_element_type=F32)


def _dot_tn(a, b):
    return lax.dot_general(a, b, (((0,), (0,)), ((), ())), preferred_element_type=F32)


def _sb_proj_kernel(x_ref, g_ref, w_ref, o_ref):
    h = _rmsnorm(x_ref[...], g_ref[...]).astype(BF16)
    for c in range(o_ref.shape[1] // COL_CHUNK):
        cols = slice(c * COL_CHUNK, (c + 1) * COL_CHUNK)
        o_ref[:, cols] = _dot(h, w_ref[:, cols]).astype(BF16)


def _sb_proj(x, gain, w):
    n = x.shape[0]
    dout = w.shape[1]
    return pl.pallas_call(
        _sb_proj_kernel,
        grid=(n // TOKEN_TILE,),
        in_specs=[
            pl.BlockSpec((TOKEN_TILE, D_MODEL), lambda i: (i, 0)),
            _const_spec((1, D_MODEL)),
            _const_spec((D_MODEL, dout)),
        ],
        out_specs=pl.BlockSpec((TOKEN_TILE, dout), lambda i: (i, 0)),
        out_shape=jax.ShapeDtypeStruct((n, dout), BF16),
        compiler_params=pltpu.CompilerParams(
            dimension_semantics=("arbitrary",), vmem_limit_bytes=VMEM_LIMIT_BYTES),
        name="sb_proj",
    )(x, gain, w)


def _sb_attn_kernel(q_ref, k_ref, v_ref, qg_ref, kg_ref, o_ref,
                    kn_ref, vcat_ref, uu_ref, carry_ref, acc_ref, *, seq):
    nblk = seq // ATT_BLOCK
    blk = ATT_BLOCK
    ng = ATT_GROUPS
    lane = lax.broadcasted_iota(jnp.int32, (1, LANES), 1)
    is_a = lane < SB_HEAD_DIM

    r = lax.broadcasted_iota(jnp.int32, (2 * LANES, LANES), 0)
    c = lax.broadcasted_iota(jnp.int32, (2 * LANES, LANES), 1)
    group_ones = (((r & (LANES - 1)) >> 6) == (c >> 6)).astype(BF16)

    def head_norm(xf, gain):
        hi, lo = _split2(xf * xf)
        ss = _dot(jnp.concatenate([hi, lo], axis=1), group_ones)
        return xf * lax.rsqrt(ss * (1.0 / SB_HEAD_DIM) + EPS) * gain

    r = lax.broadcasted_iota(jnp.int32, (2 * blk, 2 * blk), 0) & (blk - 1)
    c = lax.broadcasted_iota(jnp.int32, (2 * blk, 2 * blk), 1)
    uu_ref[...] = -((c >= blk) | (r >= c)).astype(BF16)

    def lanes_of(g):
        return slice(g * LANES, (g + 1) * LANES)

    def kprep(i, _):
        rows = pl.ds(pl.multiple_of(i * 512, 512), 512)
        for g in range(ng):
            kn_ref[g, rows, :] = head_norm(k_ref[0, rows, lanes_of(g)].astype(F32), kg_ref[...]).astype(BF16)
        return 0
    lax.fori_loop(0, seq // 512, kprep, 0)

    row2 = lax.broadcasted_iota(jnp.int32, (2 * blk, blk), 0) & (blk - 1)
    col2 = lax.broadcasted_iota(jnp.int32, (2 * blk, blk), 1)
    causal = jnp.concatenate([col2 < row2] * ng, axis=0)

    top = ATT_TOP_ROWS
    top_rows = jnp.concatenate([row2 < top] * ng, axis=0)

    def stage_logits(qqs, j, nrows=blk):
        rows = pl.ds(pl.multiple_of(j * blk, blk), blk)
        parts = []
        for g in range(ng):
            qq = qqs[g] if nrows == blk else jnp.concatenate([qqs[g][0:nrows], qqs[g][blk:blk + nrows]], axis=0)
            parts.append(_dot_nt(qq, kn_ref[g, rows, :]))
        return jnp.concatenate(parts, axis=0)

    def stage_cumsum(z):
        neg_stay = jnp.maximum(z, 0.0) + jnp.log2(1.0 + jnp.exp2(-jnp.abs(z)))
        hi, lo = _split2(neg_stay)
        cs = _dot(jnp.concatenate([hi, lo], axis=1), uu_ref[...])
        return z + cs[:, :blk], cs[:, blk:]

    def stage_weights(logw):
        return jnp.exp2(logw).astype(BF16)

    def vprep(j, _):
        rows = pl.ds(pl.multiple_of(j * blk, blk), blk)
        for g in range(ng):
            vb = v_ref[0, rows, lanes_of(g)]
            zero = jnp.zeros_like(vb)
            vcat_ref[g, j, 0:blk, :] = jnp.where(is_a, vb, zero)
            vcat_ref[g, j, blk:2 * blk, :] = jnp.where(is_a, zero, vb)
        return 0
    lax.fori_loop(0, nblk, vprep, 0)

    def value_rows(g, j):
        return vcat_ref[g, j]

    def pair_cols(wb, g, nrows=blk):
        base = g * 2 * nrows
        return jnp.concatenate([wb[base:base + nrows], wb[base + nrows:base + 2 * nrows]], axis=1)

    def head_rows(x, nrows):
        return jnp.concatenate([x[b0:b0 + nrows] for b0 in range(0, ng * 2 * blk, blk)], axis=0)

    scale = LOG2_E / math.sqrt(SB_HEAD_DIM)

    def load_queries(qi):
        rows = pl.ds(pl.multiple_of(qi * blk, blk), blk)
        qqs = []
        for g in range(ng):
            qn = head_norm(q_ref[0, rows, lanes_of(g)].astype(F32), qg_ref[...]) * scale
            qqs.append(jnp.concatenate([jnp.where(is_a, qn, 0.0), jnp.where(is_a, 0.0, qn)],
                                       axis=0).astype(BF16))
        return rows, qqs

    rows0, qqs0 = load_queries(0)
    logw, _ = stage_cumsum(jnp.where(causal, stage_logits(qqs0, 0), ATT_MASKED_LOGIT))
    wb = stage_weights(logw)
    for g in range(ng):
        o_ref[0, rows0, lanes_of(g)] = _dot(pair_cols(wb, g), value_rows(g, 0)).astype(BF16)

    def sweep(qis):
        nq = len(qis)
        loaded = [load_queries(qi) for qi in qis]
        lw_d, tot_d, lw_o, tot_o, lw_t, tot_t = [], [], [], [], [], []
        for k, qi in enumerate(qis):
            qqs = loaded[k][1]
            a, b = stage_cumsum(jnp.where(causal, stage_logits(qqs, qi), ATT_MASKED_LOGIT))
            lw_d.append(a)
            tot_d.append(b)
            a, b = stage_cumsum(stage_logits(qqs, qi - 1))
            lw_o.append(a)
            tot_o.append(b)
            z_t = jnp.where(qi >= 2, stage_logits(qqs, jnp.maximum(qi - 2, 0), top), ATT_MASKED_LOGIT)
            a, b = stage_cumsum(z_t)
            lw_t.append(a)
            tot_t.append(b)
        for k, qi in enumerate(qis):
            carry = tot_d[k] + tot_o[k]
            carry_t = head_rows(carry, top)
            wb_d = stage_weights(lw_d[k])
            wb_o = stage_weights(lw_o[k] + tot_d[k])
            wb_t = stage_weights(lw_t[k] + carry_t)
            carry_ref[k] = carry
            carry_t = carry_t + tot_t[k]
            for i, b0 in enumerate(range(0, ng * 2 * blk, blk)):
                carry_ref[k, b0:b0 + top, :] = carry_t[i * top:(i + 1) * top]
            j2 = jnp.maximum(qi - 2, 0)
            for g in range(ng):
                acc_ref[k, g] = (_dot(pair_cols(wb_d, g), value_rows(g, qi))
                                 + _dot(pair_cols(wb_o, g), value_rows(g, qi - 1)))
                acc_ref[k, g, 0:top, :] += _dot(pair_cols(wb_t, g, top), value_rows(g, j2))

        def live():
            return (jnp.max(carry_ref[...]) > ATT_SKIP_LOG * LOG2_E).astype(jnp.int32)

        def cond(state):
            d, go = state
            return jnp.logical_and(qis[-1] - 2 - d >= 0, go > 0)

        def body(state):
            d, _ = state
            for k, qi in enumerate(qis):
                j = qi - 2 - d
                z = stage_logits(loaded[k][1], jnp.maximum(j, 0))
                dead = jnp.logical_or(j < 0, jnp.logical_and(top_rows, d == 0))
                logw, tot = stage_cumsum(jnp.where(dead, ATT_MASKED_LOGIT, z))
                carry = carry_ref[k]
                wb = stage_weights(logw + carry)
                carry_ref[k] = carry + tot
                for g in range(ng):
                    acc_ref[k, g] += _dot(pair_cols(wb, g), value_rows(g, jnp.maximum(j, 0)))
            return d + 1, live()

        lax.while_loop(cond, body, (jnp.int32(0), live()))
        for k in range(nq):
            for g in range(ng):
                o_ref[0, loaded[k][0], lanes_of(g)] = acc_ref[k, g].astype(BF16)

    nfull = (nblk - 1) // ATT_SWEEP

    def sweep_step(p, _):
        sweep([1 + p * ATT_SWEEP + k for k in range(ATT_SWEEP)])
        return 0

    lax.fori_loop(0, nfull, sweep_step, 0)
    for qi in range(1 + nfull * ATT_SWEEP, nblk):
        sweep([jnp.int32(qi)])


def _sb_attn(qkv, q_gain2, k_gain2):
    bsz, seq, _ = qkv.shape
    width = ATT_GROUPS * LANES
    nsteps = D_MODEL // width
    blk = ATT_BLOCK
    kern = functools.partial(_sb_attn_kernel, seq=seq)
    return pl.pallas_call(
        kern,
        grid=(bsz, nsteps),
        in_specs=[
            pl.BlockSpec((1, seq, width), lambda b, g: (b, 0, g)),
            pl.BlockSpec((1, seq, width), lambda b, g: (b, 0, nsteps + g), pipeline_mode=pl.Buffered(1)),
            pl.BlockSpec((1, seq, width), lambda b, g: (b, 0, 2 * nsteps + g), pipeline_mode=pl.Buffered(1)),
            _const_spec((1, LANES)),
            _const_spec((1, LANES)),
        ],
        out_specs=pl.BlockSpec((1, seq, width), lambda b, g: (b, 0, g)),
        out_shape=jax.ShapeDtypeStruct((bsz, seq, D_MODEL), BF16),
        scratch_shapes=[
            pltpu.VMEM((ATT_GROUPS, seq, LANES), BF16),
            pltpu.VMEM((ATT_GROUPS, seq // blk, 2 * blk, LANES), BF16),
            pltpu.VMEM((2 * blk, 2 * blk), BF16),
            pltpu.VMEM((ATT_SWEEP, ATT_GROUPS * 2 * blk, blk), F32),
            pltpu.VMEM((ATT_SWEEP, ATT_GROUPS, blk, LANES), F32),
        ],
        compiler_params=pltpu.CompilerParams(
            dimension_semantics=("arbitrary", "arbitrary"), vmem_limit_bytes=VMEM_LIMIT_BYTES),
        name="sb_attn",
    )(qkv, qkv, qkv, q_gain2, k_gain2)


def _hg_proj_kernel(x_ref, g_ref, w_ref, lb_logits_ref, q_ref, lf_ref, i_ref, gs_ref, *, layer_j):
    h = _rmsnorm(x_ref[...], g_ref[...]).astype(BF16)

    logits = lb_logits_ref[...].astype(F32)
    p = jnp.exp(logits - jnp.max(logits, axis=0, keepdims=True))
    p = p / jnp.sum(p, axis=0, keepdims=True)
    lb = jnp.sum(p[0:layer_j + 1], axis=0, keepdims=True) - p[0:1]

    def proj(c):
        return _dot(h, w_ref[:, c * D_MODEL:(c + 1) * D_MODEL])

    q = proj(0)
    q_ref[...] = (q * jax.nn.sigmoid(q)).astype(BF16)
    forget = lb + (1.0 - lb) * jax.nn.sigmoid(proj(1))
    lf_ref[...] = jnp.log(forget)
    i_ref[...] = proj(2).astype(BF16)
    gs_ref[...] = jax.nn.sigmoid(proj(3)).astype(BF16)


def _hg_proj(x, gain, w, lb_logits, layer_j):
    n = x.shape[0]
    tile_spec = pl.BlockSpec((TOKEN_TILE, D_MODEL), lambda i: (i, 0))
    kern = functools.partial(_hg_proj_kernel, layer_j=layer_j)
    return pl.pallas_call(
        kern,
        grid=(n // TOKEN_TILE,),
        in_specs=[
            tile_spec,
            _const_spec((1, D_MODEL)),
            _const_spec((D_MODEL, 4 * D_MODEL)),
            _const_spec(lb_logits.shape),
        ],
        out_specs=[tile_spec, tile_spec, tile_spec, tile_spec],
        out_shape=[
            jax.ShapeDtypeStruct((n, D_MODEL), BF16),
            jax.ShapeDtypeStruct((n, D_MODEL), F32),
            jax.ShapeDtypeStruct((n, D_MODEL), BF16),
            jax.ShapeDtypeStruct((n, D_MODEL), BF16),
        ],
        compiler_params=pltpu.CompilerParams(
            dimension_semantics=("arbitrary",), vmem_limit_bytes=VMEM_LIMIT_BYTES),
        name="hg_proj",
    )(x, gain, w, lb_logits)


def _hg_scan_kernel(q_ref, lf_ref, v_ref, gs_ref, ng_ref, o_ref,
                    st_ref, b_scr, c_scr, *, seq_tile):
    ch, sub, nh, hd = HG_CHUNK, HG_SUB, HG_GROUP, HG_HEAD_DIM

    @pl.when(pl.program_id(2) == 0)
    def _():
        st_ref[...] = jnp.zeros_like(st_ref)

    r = lax.broadcasted_iota(jnp.int32, (ch, 3 * ch), 0)
    c = lax.broadcasted_iota(jnp.int32, (ch, 3 * ch), 1)
    tri3 = ((c % ch) <= r).astype(BF16)

    levels = []
    size = ch // 2
    while size >= sub:
        levels.append(size)
        size //= 2
    rr = lax.broadcasted_iota(jnp.int32, (ch, ch), 0)
    cc = lax.broadcasted_iota(jnp.int32, (ch, ch), 1)
    level_masks = [((rr // (2 * s)) == (cc // (2 * s))) & (((rr // s) & 1) == 1) & (((cc // s) & 1) == 0)
                   for s in levels]
    sub_row = lax.broadcasted_iota(jnp.int32, (sub, ch), 0)
    sub_col = lax.broadcasted_iota(jnp.int32, (sub, ch), 1)

    def heads(x):
        return [x[:, h * hd:(h + 1) * hd] for h in range(nh)]

    def row_bcast(ref, h, row, nrows):
        return jnp.broadcast_to(ref[h, row:row + 1, :], (nrows, hd))

    def chunk(ci, _):
        rows = pl.ds(pl.multiple_of(ci * ch, ch), ch)
        lf = lf_ref[0, rows, :]
        b = _dot(tri3, jnp.concatenate(_split3(lf), axis=0))
        b2 = heads(b * LOG2_E)
        kf = heads(jnp.maximum(1.0 - jnp.exp(lf), 0.0))
        qf = heads(q_ref[0, rows, :].astype(F32))
        vb = heads(v_ref[0, rows, :])
        gs = heads(gs_ref[0, rows, :].astype(F32))
        for h in range(nh):
            b_scr[h] = b2[h]
            c_scr[h] = b2[h] - jnp.log2(kf[h])

        out = []
        for h in range(nh):
            b2_last = row_bcast(b_scr, h, ch - 1, ch)
            st = st_ref[h]
            out.append(_dot_nt((qf[h] * jnp.exp2(b2[h])).astype(BF16), st.astype(BF16)))
            kd = (kf[h] * jnp.exp2(b2_last - b2[h])).astype(BF16)
            st_ref[h] = st * jnp.exp2(b2_last[0:1, :]) + _dot_tn(vb[h], kd)

        blocks = [[] for _ in range(nh)]
        for i in range(ch // sub):
            sl = slice(i * sub, (i + 1) * sub)
            a_blk = [jnp.zeros((sub, ch), F32) for _ in range(nh)]
            for s in range(sub):
                put = (sub_col == i * sub + s) & (sub_row >= s)
                for h in range(nh):
                    crow = row_bcast(c_scr, h, i * sub + s, sub)
                    a = jnp.sum(qf[h][sl] * jnp.exp2(b2[h][sl] - crow), axis=-1, keepdims=True)
                    a_blk[h] = jnp.where(put, a, a_blk[h])
            for h in range(nh):
                blocks[h].append(a_blk[h])
        scores = [jnp.concatenate(blocks[h], axis=0) for h in range(nh)]

        for s, mask in zip(levels, level_masks):
            for h in range(nh):
                anchor = jnp.concatenate([row_bcast(b_scr, h, p * 2 * s + s - 1, 2 * s)
                                          for p in range(ch // (2 * s))], axis=0)
                qk = jnp.concatenate([(qf[h] if (i & 1) else kf[h])[i * s:(i + 1) * s]
                                      for i in range(ch // s)], axis=0)
                x = (qk * jnp.exp2(-jnp.abs(b2[h] - anchor))).astype(BF16)
                scores[h] = jnp.where(mask, _dot_nt(x, x), scores[h])

        for h in range(nh):
            o_h = out[h] + _dot(scores[h].astype(BF16), vb[h])
            y = _rmsnorm(o_h, ng_ref[...]) * gs[h]
            o_ref[0, rows, h * hd:(h + 1) * hd] = y.astype(BF16)
        return 0

    lax.fori_loop(0, seq_tile // ch, chunk, 0, unroll=4)


def _hg_scan(q, lf, v, gs, norm_gain):
    bsz, seq, _ = q.shape
    width = HG_GROUP * HG_HEAD_DIM
    seq_tile = min(HG_SEQ_TILE, seq)
    blk_spec = pl.BlockSpec((1, seq_tile, width), lambda b, h, t: (b, t, h))
    kern = functools.partial(_hg_scan_kernel, seq_tile=seq_tile)
    chunk_scratch = pltpu.VMEM((HG_GROUP, HG_CHUNK, HG_HEAD_DIM), F32)
    return pl.pallas_call(
        kern,
        grid=(bsz, D_MODEL // width, seq // seq_tile),
        in_specs=[blk_spec, blk_spec, blk_spec, blk_spec, _const_spec((1, HG_HEAD_DIM))],
        out_specs=blk_spec,
        out_shape=jax.ShapeDtypeStruct((bsz, seq, D_MODEL), BF16),
        scratch_shapes=[
            pltpu.VMEM((HG_GROUP, HG_HEAD_DIM, HG_HEAD_DIM), F32),
            chunk_scratch, chunk_scratch,
        ],
        compiler_params=pltpu.CompilerParams(
            dimension_semantics=("arbitrary", "arbitrary", "arbitrary"), vmem_limit_bytes=VMEM_LIMIT_BYTES),
        name="hg_scan",
    )(q, lf, v, gs, norm_gain)


def _out_mlp_kernel(x_ref, y_ref, wo_ref, g_ref, w1_ref, w2_ref, o_ref):
    x1 = x_ref[...] + _dot(y_ref[...], wo_ref[...])
    h = _rmsnorm(x1, g_ref[...]).astype(BF16)
    acc = x1
    for c in range(D_FF // COL_CHUNK):
        cols = slice(c * COL_CHUNK, (c + 1) * COL_CHUNK)
        a = jnp.maximum(_dot(h, w1_ref[:, cols]), 0.0)
        acc = acc + _dot((a * a).astype(BF16), w2_ref[cols, :])
    o_ref[...] = acc


def _out_mlp(x, y, wo, gain, w1, w2):
    n = x.shape[0]
    tile_spec = pl.BlockSpec((TOKEN_TILE, D_MODEL), lambda i: (i, 0))
    return pl.pallas_call(
        _out_mlp_kernel,
        grid=(n // TOKEN_TILE,),
        in_specs=[
            tile_spec,
            tile_spec,
            _const_spec((D_MODEL, D_MODEL)),
            _const_spec((1, D_MODEL)),
            _const_spec((D_MODEL, D_FF)),
            _const_spec((D_FF, D_MODEL)),
        ],
        out_specs=tile_spec,
        out_shape=jax.ShapeDtypeStruct((n, D_MODEL), F32),
        compiler_params=pltpu.CompilerParams(
            dimension_semantics=("arbitrary",), vmem_limit_bytes=VMEM_LIMIT_BYTES),
        name="out_mlp",
    )(x, y, wo, gain, w1, w2)


def kernel(x, norm_gains, sb_w_qkv, sb_q_gain, sb_k_gain, sb_w_o, hg_w_in, hg_lb_logits, hg_norm_gain, hg_w_o, mlp_w1, mlp_w2):
    bsz, seq, d = x.shape
    depth = norm_gains.shape[0]
    n = bsz * seq
    assert d == D_MODEL and n % TOKEN_TILE == 0 and seq % 512 == 0
    xf = x.reshape(n, d)
    for layer in range(depth):
        j = layer // 2
        g_mix = norm_gains[layer, 0].reshape(1, d)
        g_mlp = norm_gains[layer, 1].reshape(1, d)
        if layer % 2 == 0:
            qkv = _sb_proj(xf, g_mix, sb_w_qkv[j].astype(BF16))
            y = _sb_attn(qkv.reshape(bsz, seq, 3 * d),
                         jnp.tile(sb_q_gain[j], 2).reshape(1, LANES),
                         jnp.tile(sb_k_gain[j], 2).reshape(1, LANES))
            wo = sb_w_o[j]
        else:
            q, lf, iv, gs = _hg_proj(xf, g_mix, hg_w_in[j].astype(BF16), hg_lb_logits, j)
            shp = (bsz, seq, d)
            y = _hg_scan(q.reshape(shp), lf.reshape(shp), iv.reshape(shp), gs.reshape(shp),
                         hg_norm_gain[j].reshape(1, HG_HEAD_DIM))
            wo = hg_w_o[j]
        xf = _out_mlp(xf, y.reshape(n, d), wo.astype(BF16), g_mlp,
                      mlp_w1[layer].astype(BF16), mlp_w2[layer].astype(BF16))
    return xf.reshape(bsz, seq, d)
```

```python
import functools
import math

import jax
import jax.numpy as jnp
from jax import lax
from jax.experimental import pallas as pl
from jax.experimental.pallas import tpu as pltpu

F32 = jnp.float32
BF16 = jnp.bfloat16

D_MODEL = 1024
D_FF = 4 * D_MODEL
EPS = 1e-6
LOG2_E = 1.4426950408889634
SB_HEAD_DIM = 64
HG_HEAD_DIM = 128
LANES = 128
VMEM_LIMIT_BYTES = 56 * 1024 * 1024

TOKEN_TILE = 512
PROJ_TILE = 1024
COL_CHUNK = 1024

ATT_BLOCK = 128
ATT_GROUPS = 4
ATT_SKIP_LOG = -90.0
ATT_MASKED_LOGIT = -1e30
ATT_TOP_ROWS = 64
ATT_SWEEP = 2

HG_CHUNK = 64
HG_SUB = 8
HG_GROUP = 8
HG_SEQ_TILE = 1024


def _const_spec(shape):
    return pl.BlockSpec(shape, lambda *_: (0,) * len(shape), pipeline_mode=pl.Buffered(1))


def _proj_tile(n):
    return PROJ_TILE if n % PROJ_TILE == 0 else TOKEN_TILE


def _rmsnorm(x, gain):
    ms = jnp.mean(x * x, axis=-1, keepdims=True)
    return x * lax.rsqrt(ms + EPS) * gain


def _split2(x):
    hi = x.astype(BF16)
    lo = (x - hi.astype(F32)).astype(BF16)
    return hi, lo


def _split3(x):
    h1 = x.astype(BF16)
    r1 = x - h1.astype(F32)
    h2 = r1.astype(BF16)
    h3 = (r1 - h2.astype(F32)).astype(BF16)
    return h1, h2, h3


def _dot(a, b):
    return jnp.dot(a, b, preferred_element_type=F32)


def _dot_nt(a, b):
    return lax.dot_general(a, b, (((1,), (1,)), ((), ())), preferred_element_type=F32)


def _dot_tn(a, b):
    return lax.dot_general(a, b, (((0,), (0,)), ((), ())), preferred_element_type=F32)


def _sb_proj_kernel(x_ref, g_ref, w_ref, o_ref):
    h = _rmsnorm(x_ref[...], g_ref[...]).astype(BF16)
    for c in range(o_ref.shape[1] // COL_CHUNK):
        cols = slice(c * COL_CHUNK, (c + 1) * COL_CHUNK)
        o_ref[:, cols] = _dot(h, w_ref[:, cols]).astype(BF16)


def _sb_proj(x, gain, w):
    n = x.shape[0]
    dout = w.shape[1]
    tile = _proj_tile(n)
    return pl.pallas_call(
        _sb_proj_kernel,
        grid=(n // tile,),
        in_specs=[
            pl.BlockSpec((tile, D_MODEL), lambda i: (i, 0)),
            _const_spec((1, D_MODEL)),
            _const_spec((D_MODEL, dout)),
        ],
        out_specs=pl.BlockSpec((tile, dout), lambda i: (i, 0)),
        out_shape=jax.ShapeDtypeStruct((n, dout), BF16),
        compiler_params=pltpu.CompilerParams(
            dimension_semantics=("arbitrary",), vmem_limit_bytes=VMEM_LIMIT_BYTES),
        name="sb_proj",
    )(x, gain, w)


def _sb_attn_kernel(q_ref, k_ref, v_ref, qg_ref, kg_ref, o_ref,
                    kn_ref, vcat_ref, uu_ref, carry_ref, acc_ref, *, seq):
    nblk = seq // ATT_BLOCK
    blk = ATT_BLOCK
    ng = ATT_GROUPS
    lane = lax.broadcasted_iota(jnp.int32, (1, LANES), 1)
    is_a = lane < SB_HEAD_DIM

    r = lax.broadcasted_iota(jnp.int32, (2 * LANES, LANES), 0)
    c = lax.broadcasted_iota(jnp.int32, (2 * LANES, LANES), 1)
    group_ones = (((r & (LANES - 1)) >> 6) == (c >> 6)).astype(BF16)

    def head_norm(xf, gain):
        hi, lo = _split2(xf * xf)
        ss = _dot(jnp.concatenate([hi, lo], axis=1), group_ones)
        return xf * lax.rsqrt(ss * (1.0 / SB_HEAD_DIM) + EPS) * gain

    r = lax.broadcasted_iota(jnp.int32, (2 * blk, 2 * blk), 0) & (blk - 1)
    c = lax.broadcasted_iota(jnp.int32, (2 * blk, 2 * blk), 1)
    uu_ref[...] = -((c >= blk) | (r >= c)).astype(BF16)

    def lanes_of(g):
        return slice(g * LANES, (g + 1) * LANES)

    def kprep(i, _):
        rows = pl.ds(pl.multiple_of(i * 512, 512), 512)
        for g in range(ng):
            kn_ref[g, rows, :] = head_norm(k_ref[0, rows, lanes_of(g)].astype(F32), kg_ref[...]).astype(BF16)
        return 0
    lax.fori_loop(0, seq // 512, kprep, 0)

    row2 = lax.broadcasted_iota(jnp.int32, (2 * blk, blk), 0) & (blk - 1)
    col2 = lax.broadcasted_iota(jnp.int32, (2 * blk, blk), 1)
    causal = jnp.concatenate([col2 < row2] * ng, axis=0)

    top = ATT_TOP_ROWS
    top_rows = jnp.concatenate([row2 < top] * ng, axis=0)

    def stage_logits(qqs, j, nrows=blk):
        rows = pl.ds(pl.multiple_of(j * blk, blk), blk)
        parts = []
        for g in range(ng):
            qq = qqs[g] if nrows == blk else jnp.concatenate([qqs[g][0:nrows], qqs[g][blk:blk + nrows]], axis=0)
            parts.append(_dot_nt(qq, kn_ref[g, rows, :]))
        return jnp.concatenate(parts, axis=0)

    def stage_cumsum(z):
        neg_stay = jnp.maximum(z, 0.0) + jnp.log2(1.0 + jnp.exp2(-jnp.abs(z)))
        hi, lo = _split2(neg_stay)
        cs = _dot(jnp.concatenate([hi, lo], axis=1), uu_ref[...])
        return z + cs[:, :blk], cs[:, blk:]

    def stage_weights(logw):
        return jnp.exp2(logw).astype(BF16)

    def vprep(j, _):
        rows = pl.ds(pl.multiple_of(j * blk, blk), blk)
        for g in range(ng):
            vb = v_ref[0, rows, lanes_of(g)]
            zero = jnp.zeros_like(vb)
            vcat_ref[g, j, 0:blk, :] = jnp.where(is_a, vb, zero)
            vcat_ref[g, j, blk:2 * blk, :] = jnp.where(is_a, zero, vb)
        return 0
    lax.fori_loop(0, nblk, vprep, 0)

    def value_rows(g, j):
        return vcat_ref[g, j]

    def pair_cols(wb, g, nrows=blk):
        base = g * 2 * nrows
        return jnp.concatenate([wb[base:base + nrows], wb[base + nrows:base + 2 * nrows]], axis=1)

    def head_rows(x, nrows):
        return jnp.concatenate([x[b0:b0 + nrows] for b0 in range(0, ng * 2 * blk, blk)], axis=0)

    scale = LOG2_E / math.sqrt(SB_HEAD_DIM)

    def load_queries(qi):
        rows = pl.ds(pl.multiple_of(qi * blk, blk), blk)
        qqs = []
        for g in range(ng):
            qn = head_norm(q_ref[0, rows, lanes_of(g)].astype(F32), qg_ref[...]) * scale
            qqs.append(jnp.concatenate([jnp.where(is_a, qn, 0.0), jnp.where(is_a, 0.0, qn)],
                                       axis=0).astype(BF16))
        return rows, qqs

    rows0, qqs0 = load_queries(0)
    logw, _ = stage_cumsum(jnp.where(causal, stage_logits(qqs0, 0), ATT_MASKED_LOGIT))
    wb = stage_weights(logw)
    for g in range(ng):
        o_ref[0, rows0, lanes_of(g)] = _dot(pair_cols(wb, g), value_rows(g, 0)).astype(BF16)

    def sweep(qis):
        nq = len(qis)
        loaded = [load_queries(qi) for qi in qis]
        lw_d, tot_d, lw_o, tot_o, lw_t, tot_t = [], [], [], [], [], []
        for k, qi in enumerate(qis):
            qqs = loaded[k][1]
            a, b = stage_cumsum(jnp.where(causal, stage_logits(qqs, qi), ATT_MASKED_LOGIT))
            lw_d.append(a)
            tot_d.append(b)
            a, b = stage_cumsum(stage_logits(qqs, qi - 1))
            lw_o.append(a)
            tot_o.append(b)
            z_t = jnp.where(qi >= 2, stage_logits(qqs, jnp.maximum(qi - 2, 0), top), ATT_MASKED_LOGIT)
            a, b = stage_cumsum(z_t)
            lw_t.append(a)
            tot_t.append(b)
        for k, qi in enumerate(qis):
            carry = tot_d[k] + tot_o[k]
            carry_t = head_rows(carry, top)
            wb_d = stage_weights(lw_d[k])
            wb_o = stage_weights(lw_o[k] + tot_d[k])
            wb_t = stage_weights(lw_t[k] + carry_t)
            carry_ref[k] = carry
            carry_t = carry_t + tot_t[k]
            for i, b0 in enumerate(range(0, ng * 2 * blk, blk)):
                carry_ref[k, b0:b0 + top, :] = carry_t[i * top:(i + 1) * top]
            j2 = jnp.maximum(qi - 2, 0)
            for g in range(ng):
                acc_ref[k, g] = (_dot(pair_cols(wb_d, g), value_rows(g, qi))
                                 + _dot(pair_cols(wb_o, g), value_rows(g, qi - 1)))
                acc_ref[k, g, 0:top, :] += _dot(pair_cols(wb_t, g, top), value_rows(g, j2))

        def live():
            return (jnp.max(carry_ref[...]) > ATT_SKIP_LOG * LOG2_E).astype(jnp.int32)

        def cond(state):
            d, go = state
            return jnp.logical_and(qis[-1] - 2 - d >= 0, go > 0)

        def body(state):
            d, _ = state
            for k, qi in enumerate(qis):
                j = qi - 2 - d
                z = stage_logits(loaded[k][1], jnp.maximum(j, 0))
                dead = jnp.logical_or(j < 0, jnp.logical_and(top_rows, d == 0))
                logw, tot = stage_cumsum(jnp.where(dead, ATT_MASKED_LOGIT, z))
                carry = carry_ref[k]
                wb = stage_weights(logw + carry)
                carry_ref[k] = carry + tot
                for g in range(ng):
                    acc_ref[k, g] += _dot(pair_cols(wb, g), value_rows(g, jnp.maximum(j, 0)))
            return d + 1, live()

        lax.while_loop(cond, body, (jnp.int32(0), live()))
        for k in range(nq):
            for g in range(ng):
                o_ref[0, loaded[k][0], lanes_of(g)] = acc_ref[k, g].astype(BF16)

    nfull = (nblk - 1) // ATT_SWEEP

    def sweep_step(p, _):
        sweep([1 + p * ATT_SWEEP + k for k in range(ATT_SWEEP)])
        return 0

    lax.fori_loop(0, nfull, sweep_step, 0)
    for qi in range(1 + nfull * ATT_SWEEP, nblk):
        sweep([jnp.int32(qi)])


def _sb_attn(qkv, q_gain2, k_gain2):
    bsz, seq, _ = qkv.shape
    width = ATT_GROUPS * LANES
    nsteps = D_MODEL // width
    blk = ATT_BLOCK
    kern = functools.partial(_sb_attn_kernel, seq=seq)
    return pl.pallas_call(
        kern,
        grid=(bsz, nsteps),
        in_specs=[
            pl.BlockSpec((1, seq, width), lambda b, g: (b, 0, g)),
            pl.BlockSpec((1, seq, width), lambda b, g: (b, 0, nsteps + g)),
            pl.BlockSpec((1, seq, width), lambda b, g: (b, 0, 2 * nsteps + g)),
            _const_spec((1, LANES)),
            _const_spec((1, LANES)),
        ],
        out_specs=pl.BlockSpec((1, seq, width), lambda b, g: (b, 0, g)),
        out_shape=jax.ShapeDtypeStruct((bsz, seq, D_MODEL), BF16),
        scratch_shapes=[
            pltpu.VMEM((ATT_GROUPS, seq, LANES), BF16),
            pltpu.VMEM((ATT_GROUPS, seq // blk, 2 * blk, LANES), BF16),
            pltpu.VMEM((2 * blk, 2 * blk), BF16),
            pltpu.VMEM((ATT_SWEEP, ATT_GROUPS * 2 * blk, blk), F32),
            pltpu.VMEM((ATT_SWEEP, ATT_GROUPS, blk, LANES), F32),
        ],
        compiler_params=pltpu.CompilerParams(
            dimension_semantics=("arbitrary", "arbitrary"), vmem_limit_bytes=VMEM_LIMIT_BYTES),
        name="sb_attn",
    )(qkv, qkv, qkv, q_gain2, k_gain2)


def _hg_proj_kernel(x_ref, g_ref, w_ref, lb_logits_ref, q_ref, lf_ref, i_ref, gs_ref, *, layer_j):
    h = _rmsnorm(x_ref[...], g_ref[...]).astype(BF16)

    logits = lb_logits_ref[...].astype(F32)
    p = jnp.exp(logits - jnp.max(logits, axis=0, keepdims=True))
    p = p / jnp.sum(p, axis=0, keepdims=True)
    lb = jnp.sum(p[0:layer_j + 1], axis=0, keepdims=True) - p[0:1]

    def proj(c):
        return _dot(h, w_ref[:, c * D_MODEL:(c + 1) * D_MODEL])

    q = proj(0)
    q_ref[...] = (q * jax.nn.sigmoid(q)).astype(BF16)
    forget = lb + (1.0 - lb) * jax.nn.sigmoid(proj(1))
    lf_ref[...] = jnp.log(forget)
    i_ref[...] = proj(2).astype(BF16)
    gs_ref[...] = jax.nn.sigmoid(proj(3)).astype(BF16)


def _hg_proj(x, gain, w, lb_logits, layer_j):
    n = x.shape[0]
    tile = _proj_tile(n)
    tile_spec = pl.BlockSpec((tile, D_MODEL), lambda i: (i, 0))
    kern = functools.partial(_hg_proj_kernel, layer_j=layer_j)
    return pl.pallas_call(
        kern,
        grid=(n // tile,),
        in_specs=[
            tile_spec,
            _const_spec((1, D_MODEL)),
            _const_spec((D_MODEL, 4 * D_MODEL)),
            _const_spec(lb_logits.shape),
        ],
        out_specs=[tile_spec, tile_spec, tile_spec, tile_spec],
        out_shape=[
            jax.ShapeDtypeStruct((n, D_MODEL), BF16),
            jax.ShapeDtypeStruct((n, D_MODEL), F32),
            jax.ShapeDtypeStruct((n, D_MODEL), BF16),
            jax.ShapeDtypeStruct((n, D_MODEL), BF16),
        ],
        compiler_params=pltpu.CompilerParams(
            dimension_semantics=("arbitrary",), vmem_limit_bytes=VMEM_LIMIT_BYTES),
        name="hg_proj",
    )(x, gain, w, lb_logits)


def _hg_scan_kernel(q_ref, lf_ref, v_ref, gs_ref, ng_ref, o_ref,
                    st_ref, b_scr, c_scr, *, seq_tile):
    ch, sub, nh, hd = HG_CHUNK, HG_SUB, HG_GROUP, HG_HEAD_DIM

    @pl.when(pl.program_id(2) == 0)
    def _():
        st_ref[...] = jnp.zeros_like(st_ref)

    r = lax.broadcasted_iota(jnp.int32, (ch, 3 * ch), 0)
    c = lax.broadcasted_iota(jnp.int32, (ch, 3 * ch), 1)
    tri3 = ((c % ch) <= r).astype(BF16)

    levels = []
    size = ch // 2
    while size >= sub:
        levels.append(size)
        size //= 2
    rr = lax.broadcasted_iota(jnp.int32, (ch, ch), 0)
    cc = lax.broadcasted_iota(jnp.int32, (ch, ch), 1)
    level_masks = [((rr // (2 * s)) == (cc // (2 * s))) & (((rr // s) & 1) == 1) & (((cc // s) & 1) == 0)
                   for s in levels]
    sub_row = lax.broadcasted_iota(jnp.int32, (sub, ch), 0)
    sub_col = lax.broadcasted_iota(jnp.int32, (sub, ch), 1)

    def heads(x):
        return [x[:, h * hd:(h + 1) * hd] for h in range(nh)]

    def row_bcast(ref, h, row, nrows):
        return jnp.broadcast_to(ref[h, row:row + 1, :], (nrows, hd))

    def chunk(ci, _):
        rows = pl.ds(pl.multiple_of(ci * ch, ch), ch)
        lf = lf_ref[0, rows, :]
        b = _dot(tri3, jnp.concatenate(_split3(lf), axis=0))
        b2 = heads(b * LOG2_E)
        kf = heads(jnp.maximum(1.0 - jnp.exp(lf), 0.0))
        qf = heads(q_ref[0, rows, :].astype(F32))
        vb = heads(v_ref[0, rows, :])
        gs = heads(gs_ref[0, rows, :].astype(F32))
        for h in range(nh):
            b_scr[h] = b2[h]
            c_scr[h] = b2[h] - jnp.log2(kf[h])

        out = []
        for h in range(nh):
            b2_last = row_bcast(b_scr, h, ch - 1, ch)
            st = st_ref[h]
            out.append(_dot_nt((qf[h] * jnp.exp2(b2[h])).astype(BF16), st.astype(BF16)))
            kd = (kf[h] * jnp.exp2(b2_last - b2[h])).astype(BF16)
            st_ref[h] = st * jnp.exp2(b2_last[0:1, :]) + _dot_tn(vb[h], kd)

        blocks = [[] for _ in range(nh)]
        for i in range(ch // sub):
            sl = slice(i * sub, (i + 1) * sub)
            a_blk = [jnp.zeros((sub, ch), F32) for _ in range(nh)]
            for s in range(sub):
                put = (sub_col == i * sub + s) & (sub_row >= s)
                for h in range(nh):
                    crow = row_bcast(c_scr, h, i * sub + s, sub)
                    a = jnp.sum(qf[h][sl] * jnp.exp2(b2[h][sl] - crow), axis=-1, keepdims=True)
                    a_blk[h] = jnp.where(put, a, a_blk[h])
            for h in range(nh):
                blocks[h].append(a_blk[h])
        scores = [jnp.concatenate(blocks[h], axis=0) for h in range(nh)]

        for s, mask in zip(levels, level_masks):
            for h in range(nh):
                anchor = jnp.concatenate([row_bcast(b_scr, h, p * 2 * s + s - 1, 2 * s)
                                          for p in range(ch // (2 * s))], axis=0)
                qk = jnp.concatenate([(qf[h] if (i & 1) else kf[h])[i * s:(i + 1) * s]
                                      for i in range(ch // s)], axis=0)
                x = (qk * jnp.exp2(-jnp.abs(b2[h] - anchor))).astype(BF16)
                scores[h] = jnp.where(mask, _dot_nt(x, x), scores[h])

        for h in range(nh):
            o_h = out[h] + _dot(scores[h].astype(BF16), vb[h])
            y = _rmsnorm(o_h, ng_ref[...]) * gs[h]
            o_ref[0, rows, h * hd:(h + 1) * hd] = y.astype(BF16)
        return 0

    lax.fori_loop(0, seq_tile // ch, chunk, 0, unroll=4)


def _hg_scan(q, lf, v, gs, norm_gain):
    bsz, seq, _ = q.shape
    width = HG_GROUP * HG_HEAD_DIM
    seq_tile = min(HG_SEQ_TILE, seq)
    blk_spec = pl.BlockSpec((1, seq_tile, width), lambda b, h, t: (b, t, h))
    kern = functools.partial(_hg_scan_kernel, seq_tile=seq_tile)
    chunk_scratch = pltpu.VMEM((HG_GROUP, HG_CHUNK, HG_HEAD_DIM), F32)
    return pl.pallas_call(
        kern,
        grid=(bsz, D_MODEL // width, seq // seq_tile),
        in_specs=[blk_spec, blk_spec, blk_spec, blk_spec, _const_spec((1, HG_HEAD_DIM))],
        out_specs=blk_spec,
        out_shape=jax.ShapeDtypeStruct((bsz, seq, D_MODEL), BF16),
        scratch_shapes=[
            pltpu.VMEM((HG_GROUP, HG_HEAD_DIM, HG_HEAD_DIM), F32),
            chunk_scratch, chunk_scratch,
        ],
        compiler_params=pltpu.CompilerParams(
            dimension_semantics=("arbitrary", "arbitrary", "arbitrary"), vmem_limit_bytes=VMEM_LIMIT_BYTES),
        name="hg_scan",
    )(q, lf, v, gs, norm_gain)


def _out_mlp_kernel(x_ref, y_ref, wo_ref, g_ref, w1_ref, w2_ref, o_ref):
    x1 = x_ref[...] + _dot(y_ref[...], wo_ref[...])
    h = _rmsnorm(x1, g_ref[...]).astype(BF16)
    acc = x1
    for c in range(D_FF // COL_CHUNK):
        cols = slice(c * COL_CHUNK, (c + 1) * COL_CHUNK)
        a = jnp.maximum(_dot(h, w1_ref[:, cols]), 0.0)
        acc = acc + _dot((a * a).astype(BF16), w2_ref[cols, :])
    o_ref[...] = acc


def _out_mlp(x, y, wo, gain, w1, w2):
    n = x.shape[0]
    tile_spec = pl.BlockSpec((TOKEN_TILE, D_MODEL), lambda i: (i, 0))
    return pl.pallas_call(
        _out_mlp_kernel,
        grid=(n // TOKEN_TILE,),
        in_specs=[
            tile_spec,
            tile_spec,
            _const_spec((D_MODEL, D_MODEL)),
            _const_spec((1, D_MODEL)),
            _const_spec((D_MODEL, D_FF)),
            _const_spec((D_FF, D_MODEL)),
        ],
        out_specs=tile_spec,
        out_shape=jax.ShapeDtypeStruct((n, D_MODEL), F32),
        compiler_params=pltpu.CompilerParams(
            dimension_semantics=("arbitrary",), vmem_limit_bytes=VMEM_LIMIT_BYTES),
        name="out_mlp",
    )(x, y, wo, gain, w1, w2)


def kernel(x, norm_gains, sb_w_qkv, sb_q_gain, sb_k_gain, sb_w_o, hg_w_in, hg_lb_logits, hg_norm_gain, hg_w_o, mlp_w1, mlp_w2):
    bsz, seq, d = x.shape
    depth = norm_gains.shape[0]
    n = bsz * seq
    assert d == D_MODEL and n % TOKEN_TILE == 0 and seq % 512 == 0
    xf = x.reshape(n, d)
    for layer in range(depth):
        j = layer // 2
        g_mix = norm_gains[layer, 0].reshape(1, d)
        g_mlp = norm_gains[layer, 1].reshape(1, d)
        if layer % 2 == 0:
            qkv = _sb_proj(xf, g_mix, sb_w_qkv[j].astype(BF16))
            y = _sb_attn(qkv.reshape(bsz, seq, 3 * d),
                         jnp.tile(sb_q_gain[j], 2).reshape(1, LANES),
                         jnp.tile(sb_k_gain[j], 2).reshape(1, LANES))
            wo = sb_w_o[j]
        else:
            q, lf, iv, gs = _hg_proj(xf, g_mix, hg_w_in[j].astype(BF16), hg_lb_logits, j)
            shp = (bsz, seq, d)
            y = _hg_scan(q.reshape(shp), lf.reshape(shp), iv.reshape(shp), gs.reshape(shp),
                         hg_norm_gain[j].reshape(1, HG_HEAD_DIM))
            wo = hg_w_o[j]
        xf = _out_mlp(xf, y.reshape(n, d), wo.astype(BF16), g_mlp,
                      mlp_w1[layer].astype(BF16), mlp_w2[layer].astype(BF16))
    return xf.reshape(bsz, seq, d)
```

```python
import functools
import math

import jax
import jax.numpy as jnp
from jax import lax
from jax.experimental import pallas as pl
from jax.experimental.pallas import tpu as pltpu

F32 = jnp.float32
BF16 = jnp.bfloat16

D_MODEL = 1024
D_FF = 4 * D_MODEL
EPS = 1e-6
LOG2_E = 1.4426950408889634
SB_HEAD_DIM = 64
HG_HEAD_DIM = 128
LANES = 128
BF16_SUBLANES = 16
VMEM_LIMIT_BYTES = 56 * 1024 * 1024

TOKEN_TILE = 512
PROJ_TILE = 1024
COL_CHUNK = 1024

ATT_BLOCK = 128
ATT_GROUPS = 4
ATT_SKIP_LOG = -90.0
ATT_MASKED_LOGIT = -1e30
ATT_TOP_ROWS = 64
ATT_SWEEP = 2

HG_CHUNK = 64
HG_SUB = 8
HG_GROUP = 8
HG_SEQ_TILE = 1024


def _const_spec(shape):
    return pl.BlockSpec(shape, lambda *_: (0,) * len(shape), pipeline_mode=pl.Buffered(1))


def _proj_tile(n):
    return PROJ_TILE if n % PROJ_TILE == 0 else TOKEN_TILE


def _rmsnorm(x, gain):
    ms = jnp.mean(x * x, axis=-1, keepdims=True)
    return x * lax.rsqrt(ms + EPS) * gain


def _split2(x):
    hi = x.astype(BF16)
    lo = (x - hi.astype(F32)).astype(BF16)
    return hi, lo


def _split3(x):
    h1 = x.astype(BF16)
    r1 = x - h1.astype(F32)
    h2 = r1.astype(BF16)
    h3 = (r1 - h2.astype(F32)).astype(BF16)
    return h1, h2, h3


def _dot(a, b):
    return jnp.dot(a, b, preferred_element_type=F32)


def _dot_nt(a, b):
    return lax.dot_general(a, b, (((1,), (1,)), ((), ())), preferred_element_type=F32)


def _dot_tn(a, b):
    return lax.dot_general(a, b, (((0,), (0,)), ((), ())), preferred_element_type=F32)


def _sb_proj_kernel(x_ref, g_ref, w_ref, o_ref):
    h = _rmsnorm(x_ref[...], g_ref[...]).astype(BF16)
    for c in range(o_ref.shape[1] // COL_CHUNK):
        cols = slice(c * COL_CHUNK, (c + 1) * COL_CHUNK)
        o_ref[:, cols] = _dot(h, w_ref[:, cols]).astype(BF16)


def _sb_proj(x, gain, w):
    n = x.shape[0]
    dout = w.shape[1]
    tile = _proj_tile(n)
    return pl.pallas_call(
        _sb_proj_kernel,
        grid=(n // tile,),
        in_specs=[
            pl.BlockSpec((tile, D_MODEL), lambda i: (i, 0)),
            _const_spec((1, D_MODEL)),
            _const_spec((D_MODEL, dout)),
        ],
        out_specs=pl.BlockSpec((tile, dout), lambda i: (i, 0)),
        out_shape=jax.ShapeDtypeStruct((n, dout), BF16),
        compiler_params=pltpu.CompilerParams(
            dimension_semantics=("arbitrary",), vmem_limit_bytes=VMEM_LIMIT_BYTES),
        name="sb_proj",
    )(x, gain, w)


def _sb_attn_kernel(q_ref, k_ref, v_ref, qg_ref, kg_ref, o_ref,
                    kn_ref, vcat_ref, uu_ref, carry_ref, acc_ref, *, seq):
    nblk = seq // ATT_BLOCK
    blk = ATT_BLOCK
    ng = ATT_GROUPS
    lane = lax.broadcasted_iota(jnp.int32, (1, LANES), 1)
    is_a = lane < SB_HEAD_DIM

    r = lax.broadcasted_iota(jnp.int32, (2 * LANES, LANES), 0)
    c = lax.broadcasted_iota(jnp.int32, (2 * LANES, LANES), 1)
    group_ones = (((r & (LANES - 1)) >> 6) == (c >> 6)).astype(BF16)

    def head_norm(xf, gain):
        hi, lo = _split2(xf * xf)
        ss = _dot(jnp.concatenate([hi, lo], axis=1), group_ones)
        return xf * lax.rsqrt(ss * (1.0 / SB_HEAD_DIM) + EPS) * gain

    r = lax.broadcasted_iota(jnp.int32, (2 * blk, 2 * blk), 0) & (blk - 1)
    c = lax.broadcasted_iota(jnp.int32, (2 * blk, 2 * blk), 1)
    uu_ref[...] = -((c >= blk) | (r >= c)).astype(BF16)

    def lanes_of(g):
        return slice(g * LANES, (g + 1) * LANES)

    def kprep(i, _):
        rows = pl.ds(pl.multiple_of(i * 512, 512), 512)
        for g in range(ng):
            kn_ref[g, rows, :] = head_norm(k_ref[0, rows, lanes_of(g)].astype(F32), kg_ref[...]).astype(BF16)
        return 0
    lax.fori_loop(0, seq // 512, kprep, 0)

    row2 = lax.broadcasted_iota(jnp.int32, (2 * blk, blk), 0) & (blk - 1)
    col2 = lax.broadcasted_iota(jnp.int32, (2 * blk, blk), 1)
    causal = jnp.concatenate([col2 < row2] * ng, axis=0)

    top = ATT_TOP_ROWS
    top_rows = jnp.concatenate([row2 < top] * ng, axis=0)

    def stage_logits(qqs, j, nrows=blk):
        rows = pl.ds(pl.multiple_of(j * blk, blk), blk)
        parts = []
        for g in range(ng):
            qq = qqs[g] if nrows == blk else jnp.concatenate([qqs[g][0:nrows], qqs[g][blk:blk + nrows]], axis=0)
            parts.append(_dot_nt(qq, kn_ref[g, rows, :]))
        return jnp.concatenate(parts, axis=0)

    def stage_cumsum(z):
        neg_stay = jnp.maximum(z, 0.0) + jnp.log2(1.0 + jnp.exp2(-jnp.abs(z)))
        hi, lo = _split2(neg_stay)
        cs = _dot(jnp.concatenate([hi, lo], axis=1), uu_ref[...])
        return z + cs[:, :blk], cs[:, blk:]

    def stage_weights(logw):
        return jnp.exp2(logw).astype(BF16)

    def vprep(j, _):
        rows = pl.ds(pl.multiple_of(j * blk, blk), blk)
        for g in range(ng):
            vb = v_ref[0, rows, lanes_of(g)]
            zero = jnp.zeros_like(vb)
            vcat_ref[g, j, 0:blk, :] = jnp.where(is_a, vb, zero)
            vcat_ref[g, j, blk:2 * blk, :] = jnp.where(is_a, zero, vb)
        return 0
    lax.fori_loop(0, nblk, vprep, 0)

    def value_rows(g, j):
        return vcat_ref[g, j]

    def pair_cols(wb, g, nrows=blk):
        base = g * 2 * nrows
        return jnp.concatenate([wb[base:base + nrows], wb[base + nrows:base + 2 * nrows]], axis=1)

    def head_rows(x, nrows):
        return jnp.concatenate([x[b0:b0 + nrows] for b0 in range(0, ng * 2 * blk, blk)], axis=0)

    scale = LOG2_E / math.sqrt(SB_HEAD_DIM)

    def load_queries(qi):
        rows = pl.ds(pl.multiple_of(qi * blk, blk), blk)
        qqs = []
        for g in range(ng):
            qn = head_norm(q_ref[0, rows, lanes_of(g)].astype(F32), qg_ref[...]) * scale
            qqs.append(jnp.concatenate([jnp.where(is_a, qn, 0.0), jnp.where(is_a, 0.0, qn)],
                                       axis=0).astype(BF16))
        return rows, qqs

    rows0, qqs0 = load_queries(0)
    logw, _ = stage_cumsum(jnp.where(causal, stage_logits(qqs0, 0), ATT_MASKED_LOGIT))
    wb = stage_weights(logw)
    for g in range(ng):
        o_ref[0, rows0, lanes_of(g)] = _dot(pair_cols(wb, g), value_rows(g, 0)).astype(BF16)

    def sweep(qis):
        nq = len(qis)
        loaded = [load_queries(qi) for qi in qis]
        lw_d, tot_d, lw_o, tot_o, lw_t, tot_t = [], [], [], [], [], []
        for k, qi in enumerate(qis):
            qqs = loaded[k][1]
            a, b = stage_cumsum(jnp.where(causal, stage_logits(qqs, qi), ATT_MASKED_LOGIT))
            lw_d.append(a)
            tot_d.append(b)
            a, b = stage_cumsum(stage_logits(qqs, qi - 1))
            lw_o.append(a)
            tot_o.append(b)
            z_t = jnp.where(qi >= 2, stage_logits(qqs, jnp.maximum(qi - 2, 0), top), ATT_MASKED_LOGIT)
            a, b = stage_cumsum(z_t)
            lw_t.append(a)
            tot_t.append(b)
        for k, qi in enumerate(qis):
            carry = tot_d[k] + tot_o[k]
            carry_t = head_rows(carry, top)
            wb_d = stage_weights(lw_d[k])
            wb_o = stage_weights(lw_o[k] + tot_d[k])
            wb_t = stage_weights(lw_t[k] + carry_t)
            carry_ref[k] = carry
            carry_t = carry_t + tot_t[k]
            for i, b0 in enumerate(range(0, ng * 2 * blk, blk)):
                carry_ref[k, b0:b0 + top, :] = carry_t[i * top:(i + 1) * top]
            j2 = jnp.maximum(qi - 2, 0)
            for g in range(ng):
                acc_ref[k, g] = (_dot(pair_cols(wb_d, g), value_rows(g, qi))
                                 + _dot(pair_cols(wb_o, g), value_rows(g, qi - 1)))
                acc_ref[k, g, 0:top, :] += _dot(pair_cols(wb_t, g, top), value_rows(g, j2))

        def live():
            return (jnp.max(carry_ref[...]) > ATT_SKIP_LOG * LOG2_E).astype(jnp.int32)

        def cond(state):
            d, go = state
            return jnp.logical_and(qis[-1] - 2 - d >= 0, go > 0)

        def body(state):
            d, _ = state
            for k, qi in enumerate(qis):
                j = qi - 2 - d
                z = stage_logits(loaded[k][1], jnp.maximum(j, 0))
                dead = jnp.logical_or(j < 0, jnp.logical_and(top_rows, d == 0))
                logw, tot = stage_cumsum(jnp.where(dead, ATT_MASKED_LOGIT, z))
                carry = carry_ref[k]
                wb = stage_weights(logw + carry)
                carry_ref[k] = carry + tot
                for g in range(ng):
                    acc_ref[k, g] += _dot(pair_cols(wb, g), value_rows(g, jnp.maximum(j, 0)))
            return d + 1, live()

        lax.while_loop(cond, body, (jnp.int32(0), live()))
        for k in range(nq):
            for g in range(ng):
                o_ref[0, loaded[k][0], lanes_of(g)] = acc_ref[k, g].astype(BF16)

    nfull = (nblk - 1) // ATT_SWEEP

    def sweep_step(p, _):
        sweep([1 + p * ATT_SWEEP + k for k in range(ATT_SWEEP)])
        return 0

    lax.fori_loop(0, nfull, sweep_step, 0)
    for qi in range(1 + nfull * ATT_SWEEP, nblk):
        sweep([jnp.int32(qi)])


def _sb_attn(qkv, q_gain2, k_gain2):
    bsz, seq, _ = qkv.shape
    width = ATT_GROUPS * LANES
    nsteps = D_MODEL // width
    blk = ATT_BLOCK
    kern = functools.partial(_sb_attn_kernel, seq=seq)
    return pl.pallas_call(
        kern,
        grid=(bsz, nsteps),
        in_specs=[
            pl.BlockSpec((1, seq, width), lambda b, g: (b, 0, g)),
            pl.BlockSpec((1, seq, width), lambda b, g: (b, 0, nsteps + g)),
            pl.BlockSpec((1, seq, width), lambda b, g: (b, 0, 2 * nsteps + g)),
            _const_spec((1, LANES)),
            _const_spec((1, LANES)),
        ],
        out_specs=pl.BlockSpec((1, seq, width), lambda b, g: (b, 0, g)),
        out_shape=jax.ShapeDtypeStruct((bsz, seq, D_MODEL), BF16),
        scratch_shapes=[
            pltpu.VMEM((ATT_GROUPS, seq, LANES), BF16),
            pltpu.VMEM((ATT_GROUPS, seq // blk, 2 * blk, LANES), BF16),
            pltpu.VMEM((2 * blk, 2 * blk), BF16),
            pltpu.VMEM((ATT_SWEEP, ATT_GROUPS * 2 * blk, blk), F32),
            pltpu.VMEM((ATT_SWEEP, ATT_GROUPS, blk, LANES), F32),
        ],
        compiler_params=pltpu.CompilerParams(
            dimension_semantics=("arbitrary", "arbitrary"), vmem_limit_bytes=VMEM_LIMIT_BYTES),
        name="sb_attn",
    )(qkv, qkv, qkv, q_gain2, k_gain2)


def _hg_proj_kernel(x_ref, g_ref, w_ref, lb_logits_ref, q_ref, lf_ref, i_ref, gs_ref, *, layer_j):
    h = _rmsnorm(x_ref[...], g_ref[...]).astype(BF16)

    logits = lb_logits_ref[...].astype(F32)
    p = jnp.exp(logits - jnp.max(logits, axis=0, keepdims=True))
    p = p / jnp.sum(p, axis=0, keepdims=True)
    lb = jnp.sum(p[0:layer_j + 1], axis=0, keepdims=True) - p[0:1]

    def proj(c):
        return _dot(h, w_ref[:, c * D_MODEL:(c + 1) * D_MODEL])

    q = proj(0)
    q_ref[...] = (q * jax.nn.sigmoid(q)).astype(BF16)
    forget = lb + (1.0 - lb) * jax.nn.sigmoid(proj(1))
    lf_ref[...] = jnp.log(forget)
    i_ref[...] = proj(2).astype(BF16)
    gs_ref[...] = jax.nn.sigmoid(proj(3)).astype(BF16)


def _hg_proj(x, gain, w, lb_logits, layer_j):
    n = x.shape[0]
    tile = _proj_tile(n)
    tile_spec = pl.BlockSpec((tile, D_MODEL), lambda i: (i, 0))
    kern = functools.partial(_hg_proj_kernel, layer_j=layer_j)
    return pl.pallas_call(
        kern,
        grid=(n // tile,),
        in_specs=[
            tile_spec,
            _const_spec((1, D_MODEL)),
            _const_spec((D_MODEL, 4 * D_MODEL)),
            _const_spec(lb_logits.shape),
        ],
        out_specs=[tile_spec, tile_spec, tile_spec, tile_spec],
        out_shape=[
            jax.ShapeDtypeStruct((n, D_MODEL), BF16),
            jax.ShapeDtypeStruct((n, D_MODEL), F32),
            jax.ShapeDtypeStruct((n, D_MODEL), BF16),
            jax.ShapeDtypeStruct((n, D_MODEL), BF16),
        ],
        compiler_params=pltpu.CompilerParams(
            dimension_semantics=("arbitrary",), vmem_limit_bytes=VMEM_LIMIT_BYTES),
        name="hg_proj",
    )(x, gain, w, lb_logits)


def _hg_scan_kernel(q_ref, lf_ref, v_ref, gs_ref, ng_ref, o_ref,
                    st_ref, b_scr, c_scr, *, seq_tile):
    ch, sub, nh, hd = HG_CHUNK, HG_SUB, HG_GROUP, HG_HEAD_DIM

    @pl.when(pl.program_id(2) == 0)
    def _():
        st_ref[...] = jnp.zeros_like(st_ref)

    r = lax.broadcasted_iota(jnp.int32, (ch, 3 * ch), 0)
    c = lax.broadcasted_iota(jnp.int32, (ch, 3 * ch), 1)
    tri3 = ((c % ch) <= r).astype(BF16)

    levels = []
    size = ch // 2
    while size >= sub:
        levels.append(size)
        size //= 2
    rr = lax.broadcasted_iota(jnp.int32, (ch, ch), 0)
    cc = lax.broadcasted_iota(jnp.int32, (ch, ch), 1)
    level_masks = [((rr // (2 * s)) == (cc // (2 * s))) & (((rr // s) & 1) == 1) & (((cc // s) & 1) == 0)
                   for s in levels]
    sub_row = lax.broadcasted_iota(jnp.int32, (sub, ch), 0)
    sub_col = lax.broadcasted_iota(jnp.int32, (sub, ch), 1)

    def heads(x):
        return [x[:, h * hd:(h + 1) * hd] for h in range(nh)]

    def row_bcast(ref, h, row, nrows):
        return jnp.broadcast_to(ref[h, row:row + 1, :], (nrows, hd))

    def chunk(ci, _):
        rows = pl.ds(pl.multiple_of(ci * ch, ch), ch)
        lf = lf_ref[0, rows, :]
        b = _dot(tri3, jnp.concatenate(_split3(lf), axis=0))
        b2 = heads(b * LOG2_E)
        kf = heads(jnp.maximum(1.0 - jnp.exp(lf), 0.0))
        qf = heads(q_ref[0, rows, :].astype(F32))
        vb = heads(v_ref[0, rows, :])
        gs = heads(gs_ref[0, rows, :].astype(F32))
        for h in range(nh):
            b_scr[h] = b2[h]
            c_scr[h] = b2[h] - jnp.log2(kf[h])

        out = []
        for h in range(nh):
            b2_last = row_bcast(b_scr, h, ch - 1, ch)
            st = st_ref[h]
            out.append(_dot_nt((qf[h] * jnp.exp2(b2[h])).astype(BF16), st.astype(BF16)))
            kd = (kf[h] * jnp.exp2(b2_last - b2[h])).astype(BF16)
            st_ref[h] = st * jnp.exp2(b2_last[0:1, :]) + _dot_tn(vb[h], kd)

        blocks = [[] for _ in range(nh)]
        for i in range(ch // sub):
            sl = slice(i * sub, (i + 1) * sub)
            a_blk = [jnp.zeros((sub, ch), F32) for _ in range(nh)]
            for s in range(sub):
                put = (sub_col == i * sub + s) & (sub_row >= s)
                for h in range(nh):
                    crow = row_bcast(c_scr, h, i * sub + s, sub)
                    a = jnp.sum(qf[h][sl] * jnp.exp2(b2[h][sl] - crow), axis=-1, keepdims=True)
                    a_blk[h] = jnp.where(put, a, a_blk[h])
            for h in range(nh):
                blocks[h].append(a_blk[h])
        scores = [jnp.concatenate(blocks[h], axis=0) for h in range(nh)]

        for s, mask in zip(levels, level_masks):
            for h in range(nh):
                anchor = jnp.concatenate([row_bcast(b_scr, h, p * 2 * s + s - 1, 2 * s)
                                          for p in range(ch // (2 * s))], axis=0)
                qk = jnp.concatenate([(qf[h] if (i & 1) else kf[h])[i * s:(i + 1) * s]
                                      for i in range(ch // s)], axis=0)
                x = (qk * jnp.exp2(-jnp.abs(b2[h] - anchor))).astype(BF16)
                scores[h] = jnp.where(mask, _dot_nt(x, x), scores[h])

        for h in range(nh):
            o_h = out[h] + _dot(scores[h].astype(BF16), vb[h])
            y = _rmsnorm(o_h, ng_ref[...]) * gs[h]
            o_ref[0, rows, h * hd:(h + 1) * hd] = y.astype(BF16)
        return 0

    lax.fori_loop(0, seq_tile // ch, chunk, 0, unroll=4)


def _hg_scan(q, lf, v, gs, norm_gain):
    bsz, seq, _ = q.shape
    width = HG_GROUP * HG_HEAD_DIM
    seq_tile = min(HG_SEQ_TILE, seq)
    blk_spec = pl.BlockSpec((1, seq_tile, width), lambda b, h, t: (b, t, h))
    kern = functools.partial(_hg_scan_kernel, seq_tile=seq_tile)
    chunk_scratch = pltpu.VMEM((HG_GROUP, HG_CHUNK, HG_HEAD_DIM), F32)
    return pl.pallas_call(
        kern,
        grid=(bsz, D_MODEL // width, seq // seq_tile),
        in_specs=[blk_spec, blk_spec, blk_spec, blk_spec, _const_spec((1, HG_HEAD_DIM))],
        out_specs=blk_spec,
        out_shape=jax.ShapeDtypeStruct((bsz, seq, D_MODEL), BF16),
        scratch_shapes=[
            pltpu.VMEM((HG_GROUP, HG_HEAD_DIM, HG_HEAD_DIM), F32),
            chunk_scratch, chunk_scratch,
        ],
        compiler_params=pltpu.CompilerParams(
            dimension_semantics=("arbitrary", "arbitrary", "arbitrary"), vmem_limit_bytes=VMEM_LIMIT_BYTES),
        name="hg_scan",
    )(q, lf, v, gs, norm_gain)


def _out_mlp_kernel(*refs, n_cast):
    x_ref, y_ref, wo_ref, g_ref, w1_ref, w2_ref = refs[:6]
    cast_src = refs[6:6 + n_cast]
    o_ref = refs[6 + n_cast]
    cast_dst = refs[7 + n_cast:]
    x1 = x_ref[...] + _dot(y_ref[...], wo_ref[...])
    h = _rmsnorm(x1, g_ref[...]).astype(BF16)
    acc = x1
    for c in range(D_FF // COL_CHUNK):
        cols = slice(c * COL_CHUNK, (c + 1) * COL_CHUNK)
        a = jnp.maximum(_dot(h, w1_ref[:, cols]), 0.0)
        acc = acc + _dot((a * a).astype(BF16), w2_ref[cols, :])
    o_ref[...] = acc
    for src, dst in zip(cast_src, cast_dst):
        dst[...] = src[...].astype(BF16)


def _can_cast_in_steps(weights, steps):
    return all(w.shape[0] % steps == 0 and (w.shape[0] // steps) % BF16_SUBLANES == 0 for w in weights)


def _out_mlp(x, y, wo, gain, w1, w2, next_weights=()):
    n = x.shape[0]
    steps = n // TOKEN_TILE
    tile_spec = pl.BlockSpec((TOKEN_TILE, D_MODEL), lambda i: (i, 0))
    slab_specs = [pl.BlockSpec((w.shape[0] // steps, w.shape[1]), lambda i: (i, 0)) for w in next_weights]
    outs = pl.pallas_call(
        functools.partial(_out_mlp_kernel, n_cast=len(next_weights)),
        grid=(steps,),
        in_specs=[
            tile_spec,
            tile_spec,
            _const_spec((D_MODEL, D_MODEL)),
            _const_spec((1, D_MODEL)),
            _const_spec((D_MODEL, D_FF)),
            _const_spec((D_FF, D_MODEL)),
        ] + slab_specs,
        out_specs=[tile_spec] + slab_specs,
        out_shape=[jax.ShapeDtypeStruct((n, D_MODEL), F32)]
        + [jax.ShapeDtypeStruct(w.shape, BF16) for w in next_weights],
        compiler_params=pltpu.CompilerParams(
            dimension_semantics=("arbitrary",), vmem_limit_bytes=VMEM_LIMIT_BYTES),
        name="out_mlp",
    )(x, y, wo, gain, w1, w2, *next_weights)
    return outs[0], list(outs[1:])


def kernel(x, norm_gains, sb_w_qkv, sb_q_gain, sb_k_gain, sb_w_o, hg_w_in, hg_lb_logits, hg_norm_gain, hg_w_o, mlp_w1, mlp_w2):
    bsz, seq, d = x.shape
    depth = norm_gains.shape[0]
    n = bsz * seq
    assert d == D_MODEL and n % TOKEN_TILE == 0 and seq % 512 == 0
    xf = x.reshape(n, d)

    def layer_weights(layer):
        j = layer // 2
        if layer % 2 == 0:
            return [sb_w_qkv[j], sb_w_o[j], mlp_w1[layer], mlp_w2[layer]]
        return [hg_w_in[j], hg_w_o[j], mlp_w1[layer], mlp_w2[layer]]

    w_in, w_out, w1, w2 = [w.astype(BF16) for w in layer_weights(0)]
    for layer in range(depth):
        j = layer // 2
        g_mix = norm_gains[layer, 0].reshape(1, d)
        g_mlp = norm_gains[layer, 1].reshape(1, d)
        if layer % 2 == 0:
            qkv = _sb_proj(xf, g_mix, w_in)
            y = _sb_attn(qkv.reshape(bsz, seq, 3 * d),
                         jnp.tile(sb_q_gain[j], 2).reshape(1, LANES),
                         jnp.tile(sb_k_gain[j], 2).reshape(1, LANES))
        else:
            q, lf, iv, gs = _hg_proj(xf, g_mix, w_in, hg_lb_logits, j)
            shp = (bsz, seq, d)
            y = _hg_scan(q.reshape(shp), lf.reshape(shp), iv.reshape(shp), gs.reshape(shp),
                         hg_norm_gain[j].reshape(1, HG_HEAD_DIM))
        nxt = layer_weights(layer + 1) if layer + 1 < depth else []
        in_kernel = _can_cast_in_steps(nxt, n // TOKEN_TILE)
        xf, cast = _out_mlp(xf, y.reshape(n, d), w_out, g_mlp, w1, w2, nxt if in_kernel else ())
        if nxt:
            w_in, w_out, w1, w2 = cast if in_kernel else [w.astype(BF16) for w in nxt]
    return xf.reshape(bsz, seq, d)
```

```python
import functools
import math

import jax
import jax.numpy as jnp
from jax import lax
from jax.experimental import pallas as pl
from jax.experimental.pallas import tpu as pltpu

F32 = jnp.float32
BF16 = jnp.bfloat16

D_MODEL = 1024
D_FF = 4 * D_MODEL
EPS = 1e-6
LOG2_E = 1.4426950408889634
SB_HEAD_DIM = 64
HG_HEAD_DIM = 128
LANES = 128
BF16_SUBLANES = 16
VMEM_LIMIT_BYTES = 56 * 1024 * 1024

TOKEN_TILE = 512
PROJ_TILE = 1024
COL_CHUNK = 1024

ATT_BLOCK = 128
ATT_GROUPS = 4
ATT_SKIP_LOG = -90.0
ATT_MASKED_LOGIT = -1e30
ATT_TOP_ROWS = 64
ATT_SWEEP = 2

HG_CHUNK = 64
HG_SUB = 8
HG_GROUP = 8
HG_SEQ_TILE = 1024


def _const_spec(shape):
    return pl.BlockSpec(shape, lambda *_: (0,) * len(shape), pipeline_mode=pl.Buffered(1))


def _proj_tile(n):
    return PROJ_TILE if n % PROJ_TILE == 0 else TOKEN_TILE


def _rmsnorm(x, gain):
    ms = jnp.mean(x * x, axis=-1, keepdims=True)
    return x * lax.rsqrt(ms + EPS) * gain


def _split2(x):
    hi = x.astype(BF16)
    lo = (x - hi.astype(F32)).astype(BF16)
    return hi, lo


def _split3(x):
    h1 = x.astype(BF16)
    r1 = x - h1.astype(F32)
    h2 = r1.astype(BF16)
    h3 = (r1 - h2.astype(F32)).astype(BF16)
    return h1, h2, h3


def _dot(a, b):
    return jnp.dot(a, b, preferred_element_type=F32)


def _dot_nt(a, b):
    return lax.dot_general(a, b, (((1,), (1,)), ((), ())), preferred_element_type=F32)


def _dot_tn(a, b):
    return lax.dot_general(a, b, (((0,), (0,)), ((), ())), preferred_element_type=F32)


def _sb_proj_kernel(x_ref, g_ref, w_ref, o_ref):
    h = _rmsnorm(x_ref[...], g_ref[...]).astype(BF16)
    for c in range(o_ref.shape[1] // COL_CHUNK):
        cols = slice(c * COL_CHUNK, (c + 1) * COL_CHUNK)
        o_ref[:, cols] = _dot(h, w_ref[:, cols]).astype(BF16)


def _sb_proj(x, gain, w):
    n = x.shape[0]
    dout = w.shape[1]
    tile = _proj_tile(n)
    return pl.pallas_call(
        _sb_proj_kernel,
        grid=(n // tile,),
        in_specs=[
            pl.BlockSpec((tile, D_MODEL), lambda i: (i, 0)),
            _const_spec((1, D_MODEL)),
            _const_spec((D_MODEL, dout)),
        ],
        out_specs=pl.BlockSpec((tile, dout), lambda i: (i, 0)),
        out_shape=jax.ShapeDtypeStruct((n, dout), BF16),
        compiler_params=pltpu.CompilerParams(
            dimension_semantics=("arbitrary",), vmem_limit_bytes=VMEM_LIMIT_BYTES),
        name="sb_proj",
    )(x, gain, w)


def _sb_attn_kernel(q_ref, k_ref, v_ref, qg_ref, kg_ref, o_ref,
                    kn_ref, vcat_ref, uu_ref, carry_ref, acc_ref, *, seq):
    nblk = seq // ATT_BLOCK
    blk = ATT_BLOCK
    ng = ATT_GROUPS
    lane = lax.broadcasted_iota(jnp.int32, (1, LANES), 1)
    is_a = lane < SB_HEAD_DIM

    r = lax.broadcasted_iota(jnp.int32, (2 * LANES, LANES), 0)
    c = lax.broadcasted_iota(jnp.int32, (2 * LANES, LANES), 1)
    group_ones = (((r & (LANES - 1)) >> 6) == (c >> 6)).astype(BF16)

    def head_norm(xf, gain):
        hi, lo = _split2(xf * xf)
        ss = _dot(jnp.concatenate([hi, lo], axis=1), group_ones)
        return xf * lax.rsqrt(ss * (1.0 / SB_HEAD_DIM) + EPS) * gain

    r = lax.broadcasted_iota(jnp.int32, (2 * blk, 2 * blk), 0) & (blk - 1)
    c = lax.broadcasted_iota(jnp.int32, (2 * blk, 2 * blk), 1)
    uu_ref[...] = -((c >= blk) | (r >= c)).astype(BF16)

    def lanes_of(g):
        return slice(g * LANES, (g + 1) * LANES)

    def kprep(i, _):
        rows = pl.ds(pl.multiple_of(i * 512, 512), 512)
        for g in range(ng):
            kn_ref[g, rows, :] = head_norm(k_ref[0, rows, lanes_of(g)].astype(F32), kg_ref[...]).astype(BF16)
        return 0
    lax.fori_loop(0, seq // 512, kprep, 0)

    row2 = lax.broadcasted_iota(jnp.int32, (2 * blk, blk), 0) & (blk - 1)
    col2 = lax.broadcasted_iota(jnp.int32, (2 * blk, blk), 1)
    causal = jnp.concatenate([col2 < row2] * ng, axis=0)

    top = ATT_TOP_ROWS
    top_rows = jnp.concatenate([row2 < top] * ng, axis=0)

    def stage_logits(qqs, j, nrows=blk):
        rows = pl.ds(pl.multiple_of(j * blk, blk), blk)
        parts = []
        for g in range(ng):
            qq = qqs[g] if nrows == blk else jnp.concatenate([qqs[g][0:nrows], qqs[g][blk:blk + nrows]], axis=0)
            parts.append(_dot_nt(qq, kn_ref[g, rows, :]))
        return jnp.concatenate(parts, axis=0)

    def stage_cumsum(z):
        neg_stay = jnp.maximum(z, 0.0) + jnp.log2(1.0 + jnp.exp2(-jnp.abs(z)))
        hi, lo = _split2(neg_stay)
        cs = _dot(jnp.concatenate([hi, lo], axis=1), uu_ref[...])
        return z + cs[:, :blk], cs[:, blk:]

    def stage_weights(logw):
        return jnp.exp2(logw).astype(BF16)

    def vprep(j, _):
        rows = pl.ds(pl.multiple_of(j * blk, blk), blk)
        for g in range(ng):
            vb = v_ref[0, rows, lanes_of(g)]
            zero = jnp.zeros_like(vb)
            vcat_ref[g, j, 0:blk, :] = jnp.where(is_a, vb, zero)
            vcat_ref[g, j, blk:2 * blk, :] = jnp.where(is_a, zero, vb)
        return 0
    lax.fori_loop(0, nblk, vprep, 0)

    def value_rows(g, j):
        return vcat_ref[g, j]

    def pair_cols(wb, g, nrows=blk):
        base = g * 2 * nrows
        return jnp.concatenate([wb[base:base + nrows], wb[base + nrows:base + 2 * nrows]], axis=1)

    def head_rows(x, nrows):
        return jnp.concatenate([x[b0:b0 + nrows] for b0 in range(0, ng * 2 * blk, blk)], axis=0)

    scale = LOG2_E / math.sqrt(SB_HEAD_DIM)

    def load_queries(qi):
        rows = pl.ds(pl.multiple_of(qi * blk, blk), blk)
        qqs = []
        for g in range(ng):
            qn = head_norm(q_ref[0, rows, lanes_of(g)].astype(F32), qg_ref[...]) * scale
            qqs.append(jnp.concatenate([jnp.where(is_a, qn, 0.0), jnp.where(is_a, 0.0, qn)],
                                       axis=0).astype(BF16))
        return rows, qqs

    rows0, qqs0 = load_queries(0)
    logw, _ = stage_cumsum(jnp.where(causal, stage_logits(qqs0, 0), ATT_MASKED_LOGIT))
    wb = stage_weights(logw)
    for g in range(ng):
        o_ref[0, rows0, lanes_of(g)] = _dot(pair_cols(wb, g), value_rows(g, 0)).astype(BF16)

    def sweep(qis):
        nq = len(qis)
        loaded = [load_queries(qi) for qi in qis]
        lw_d, tot_d, lw_o, tot_o, lw_t, tot_t = [], [], [], [], [], []
        for k, qi in enumerate(qis):
            qqs = loaded[k][1]
            a, b = stage_cumsum(jnp.where(causal, stage_logits(qqs, qi), ATT_MASKED_LOGIT))
            lw_d.append(a)
            tot_d.append(b)
            a, b = stage_cumsum(stage_logits(qqs, qi - 1))
            lw_o.append(a)
            tot_o.append(b)
            z_t = jnp.where(qi >= 2, stage_logits(qqs, jnp.maximum(qi - 2, 0), top), ATT_MASKED_LOGIT)
            a, b = stage_cumsum(z_t)
            lw_t.append(a)
            tot_t.append(b)
        for k, qi in enumerate(qis):
            carry = tot_d[k] + tot_o[k]
            carry_t = head_rows(carry, top)
            wb_d = stage_weights(lw_d[k])
            wb_o = stage_weights(lw_o[k] + tot_d[k])
            wb_t = stage_weights(lw_t[k] + carry_t)
            carry_ref[k] = carry
            carry_t = carry_t + tot_t[k]
            for i, b0 in enumerate(range(0, ng * 2 * blk, blk)):
                carry_ref[k, b0:b0 + top, :] = carry_t[i * top:(i + 1) * top]
            j2 = jnp.maximum(qi - 2, 0)
            for g in range(ng):
                acc_ref[k, g] = (_dot(pair_cols(wb_d, g), value_rows(g, qi))
                                 + _dot(pair_cols(wb_o, g), value_rows(g, qi - 1)))
                acc_ref[k, g, 0:top, :] += _dot(pair_cols(wb_t, g, top), value_rows(g, j2))

        def live():
            return (jnp.max(carry_ref[...]) > ATT_SKIP_LOG * LOG2_E).astype(jnp.int32)

        def cond(state):
            d, go = state
            return jnp.logical_and(qis[-1] - 2 - d >= 0, go > 0)

        def body(state):
            d, _ = state
            for k, qi in enumerate(qis):
                j = qi - 2 - d
                z = stage_logits(loaded[k][1], jnp.maximum(j, 0))
                dead = jnp.logical_or(j < 0, jnp.logical_and(top_rows, d == 0))
                logw, tot = stage_cumsum(jnp.where(dead, ATT_MASKED_LOGIT, z))
                carry = carry_ref[k]
                wb = stage_weights(logw + carry)
                carry_ref[k] = carry + tot
                for g in range(ng):
                    acc_ref[k, g] += _dot(pair_cols(wb, g), value_rows(g, jnp.maximum(j, 0)))
            return d + 1, live()

        lax.while_loop(cond, body, (jnp.int32(0), live()))
        for k in range(nq):
            for g in range(ng):
                o_ref[0, loaded[k][0], lanes_of(g)] = acc_ref[k, g].astype(BF16)

    nfull = (nblk - 1) // ATT_SWEEP

    def sweep_step(p, _):
        sweep([1 + p * ATT_SWEEP + k for k in range(ATT_SWEEP)])
        return 0

    lax.fori_loop(0, nfull, sweep_step, 0)
    for qi in range(1 + nfull * ATT_SWEEP, nblk):
        sweep([jnp.int32(qi)])


def _sb_attn(qkv, q_gain2, k_gain2):
    bsz, seq, _ = qkv.shape
    width = ATT_GROUPS * LANES
    nsteps = D_MODEL // width
    blk = ATT_BLOCK
    kern = functools.partial(_sb_attn_kernel, seq=seq)
    return pl.pallas_call(
        kern,
        grid=(bsz, nsteps),
        in_specs=[
            pl.BlockSpec((1, seq, width), lambda b, g: (b, 0, g)),
            pl.BlockSpec((1, seq, width), lambda b, g: (b, 0, nsteps + g)),
            pl.BlockSpec((1, seq, width), lambda b, g: (b, 0, 2 * nsteps + g)),
            _const_spec((1, LANES)),
            _const_spec((1, LANES)),
        ],
        out_specs=pl.BlockSpec((1, seq, width), lambda b, g: (b, 0, g)),
        out_shape=jax.ShapeDtypeStruct((bsz, seq, D_MODEL), BF16),
        scratch_shapes=[
            pltpu.VMEM((ATT_GROUPS, seq, LANES), BF16),
            pltpu.VMEM((ATT_GROUPS, seq // blk, 2 * blk, LANES), BF16),
            pltpu.VMEM((2 * blk, 2 * blk), BF16),
            pltpu.VMEM((ATT_SWEEP, ATT_GROUPS * 2 * blk, blk), F32),
            pltpu.VMEM((ATT_SWEEP, ATT_GROUPS, blk, LANES), F32),
        ],
        compiler_params=pltpu.CompilerParams(
            dimension_semantics=("arbitrary", "arbitrary"), vmem_limit_bytes=VMEM_LIMIT_BYTES),
        name="sb_attn",
    )(qkv, qkv, qkv, q_gain2, k_gain2)


def _hg_proj_kernel(x_ref, g_ref, w_ref, lb_logits_ref, q_ref, lf_ref, i_ref, gs_ref, *, layer_j):
    h = _rmsnorm(x_ref[...], g_ref[...]).astype(BF16)

    logits = lb_logits_ref[...].astype(F32)
    p = jnp.exp(logits - jnp.max(logits, axis=0, keepdims=True))
    p = p / jnp.sum(p, axis=0, keepdims=True)
    lb = jnp.sum(p[0:layer_j + 1], axis=0, keepdims=True) - p[0:1]

    def proj(c):
        return _dot(h, w_ref[:, c * D_MODEL:(c + 1) * D_MODEL])

    q = proj(0)
    q_ref[...] = (q * jax.nn.sigmoid(q)).astype(BF16)
    forget = lb + (1.0 - lb) * jax.nn.sigmoid(proj(1))
    lf_ref[...] = jnp.log(forget)
    i_ref[...] = proj(2).astype(BF16)
    gs_ref[...] = jax.nn.sigmoid(proj(3)).astype(BF16)


def _hg_proj(x, gain, w, lb_logits, layer_j):
    n = x.shape[0]
    tile = _proj_tile(n)
    tile_spec = pl.BlockSpec((tile, D_MODEL), lambda i: (i, 0))
    kern = functools.partial(_hg_proj_kernel, layer_j=layer_j)
    return pl.pallas_call(
        kern,
        grid=(n // tile,),
        in_specs=[
            tile_spec,
            _const_spec((1, D_MODEL)),
            _const_spec((D_MODEL, 4 * D_MODEL)),
            _const_spec(lb_logits.shape),
        ],
        out_specs=[tile_spec, tile_spec, tile_spec, tile_spec],
        out_shape=[
            jax.ShapeDtypeStruct((n, D_MODEL), BF16),
            jax.ShapeDtypeStruct((n, D_MODEL), F32),
            jax.ShapeDtypeStruct((n, D_MODEL), BF16),
            jax.ShapeDtypeStruct((n, D_MODEL), BF16),
        ],
        compiler_params=pltpu.CompilerParams(
            dimension_semantics=("arbitrary",), vmem_limit_bytes=VMEM_LIMIT_BYTES),
        name="hg_proj",
    )(x, gain, w, lb_logits)


def _hg_scan_kernel(q_ref, lf_ref, v_ref, gs_ref, ng_ref, o_ref,
                    st_ref, b_scr, c_scr, *, seq_tile):
    ch, sub, nh, hd = HG_CHUNK, HG_SUB, HG_GROUP, HG_HEAD_DIM

    @pl.when(pl.program_id(2) == 0)
    def _():
        st_ref[...] = jnp.zeros_like(st_ref)

    r = lax.broadcasted_iota(jnp.int32, (ch, 3 * ch), 0)
    c = lax.broadcasted_iota(jnp.int32, (ch, 3 * ch), 1)
    tri3 = ((c % ch) <= r).astype(BF16)

    levels = []
    size = ch // 2
    while size >= sub:
        levels.append(size)
        size //= 2
    rr = lax.broadcasted_iota(jnp.int32, (ch, ch), 0)
    cc = lax.broadcasted_iota(jnp.int32, (ch, ch), 1)
    level_masks = [((rr // (2 * s)) == (cc // (2 * s))) & (((rr // s) & 1) == 1) & (((cc // s) & 1) == 0)
                   for s in levels]
    sub_row = lax.broadcasted_iota(jnp.int32, (sub, ch), 0)
    sub_col = lax.broadcasted_iota(jnp.int32, (sub, ch), 1)

    def heads(x):
        return [x[:, h * hd:(h + 1) * hd] for h in range(nh)]

    def row_bcast(ref, h, row, nrows):
        return jnp.broadcast_to(ref[h, row:row + 1, :], (nrows, hd))

    def chunk(ci, _):
        rows = pl.ds(pl.multiple_of(ci * ch, ch), ch)
        lf = lf_ref[0, rows, :]
        b = _dot(tri3, jnp.concatenate(_split3(lf), axis=0))
        b2 = heads(b * LOG2_E)
        kf = heads(jnp.maximum(1.0 - jnp.exp(lf), 0.0))
        qf = heads(q_ref[0, rows, :].astype(F32))
        vb = heads(v_ref[0, rows, :])
        gs = heads(gs_ref[0, rows, :].astype(F32))
        for h in range(nh):
            b_scr[h] = b2[h]
            c_scr[h] = b2[h] - jnp.log2(kf[h])

        out = []
        for h in range(nh):
            b2_last = row_bcast(b_scr, h, ch - 1, ch)
            st = st_ref[h]
            out.append(_dot_nt((qf[h] * jnp.exp2(b2[h])).astype(BF16), st.astype(BF16)))
            kd = (kf[h] * jnp.exp2(b2_last - b2[h])).astype(BF16)
            st_ref[h] = st * jnp.exp2(b2_last[0:1, :]) + _dot_tn(vb[h], kd)

        blocks = [[] for _ in range(nh)]
        for i in range(ch // sub):
            sl = slice(i * sub, (i + 1) * sub)
            a_blk = [jnp.zeros((sub, ch), F32) for _ in range(nh)]
            for s in range(sub):
                put = (sub_col == i * sub + s) & (sub_row >= s)
                for h in range(nh):
                    crow = row_bcast(c_scr, h, i * sub + s, sub)
                    a = jnp.sum(qf[h][sl] * jnp.exp2(b2[h][sl] - crow), axis=-1, keepdims=True)
                    a_blk[h] = jnp.where(put, a, a_blk[h])
            for h in range(nh):
                blocks[h].append(a_blk[h])
        scores = [jnp.concatenate(blocks[h], axis=0) for h in range(nh)]

        for s, mask in zip(levels, level_masks):
            for h in range(nh):
                anchor = jnp.concatenate([row_bcast(b_scr, h, p * 2 * s + s - 1, 2 * s)
                                          for p in range(ch // (2 * s))], axis=0)
                qk = jnp.concatenate([(qf[h] if (i & 1) else kf[h])[i * s:(i + 1) * s]
                                      for i in range(ch // s)], axis=0)
                x = (qk * jnp.exp2(-jnp.abs(b2[h] - anchor))).astype(BF16)
                scores[h] = jnp.where(mask, _dot_nt(x, x), scores[h])

        for h in range(nh):
            o_h = out[h] + _dot(scores[h].astype(BF16), vb[h])
            y = _rmsnorm(o_h, ng_ref[...]) * gs[h]
            o_ref[0, rows, h * hd:(h + 1) * hd] = y.astype(BF16)
        return 0

    lax.fori_loop(0, seq_tile // ch, chunk, 0, unroll=4)


def _hg_scan(q, lf, v, gs, norm_gain):
    bsz, seq, _ = q.shape
    width = HG_GROUP * HG_HEAD_DIM
    seq_tile = min(HG_SEQ_TILE, seq)
    blk_spec = pl.BlockSpec((1, seq_tile, width), lambda b, h, t: (b, t, h))
    kern = functools.partial(_hg_scan_kernel, seq_tile=seq_tile)
    chunk_scratch = pltpu.VMEM((HG_GROUP, HG_CHUNK, HG_HEAD_DIM), F32)
    return pl.pallas_call(
        kern,
        grid=(bsz, D_MODEL // width, seq // seq_tile),
        in_specs=[blk_spec, blk_spec, blk_spec, blk_spec, _const_spec((1, HG_HEAD_DIM))],
        out_specs=blk_spec,
        out_shape=jax.ShapeDtypeStruct((bsz, seq, D_MODEL), BF16),
        scratch_shapes=[
            pltpu.VMEM((HG_GROUP, HG_HEAD_DIM, HG_HEAD_DIM), F32),
            chunk_scratch, chunk_scratch,
        ],
        compiler_params=pltpu.CompilerParams(
            dimension_semantics=("arbitrary", "arbitrary", "arbitrary"), vmem_limit_bytes=VMEM_LIMIT_BYTES),
        name="hg_scan",
    )(q, lf, v, gs, norm_gain)


def _out_mlp_kernel(*refs, n_cast):
    x_ref, y_ref, wo_ref, g_ref, w1_ref, w2_ref = refs[:6]
    cast_src = refs[6:6 + n_cast]
    o_ref = refs[6 + n_cast]
    cast_dst = refs[7 + n_cast:]
    x1 = x_ref[...] + _dot(y_ref[...], wo_ref[...])
    h = _rmsnorm(x1, g_ref[...]).astype(BF16)
    acc = x1
    for c in range(D_FF // COL_CHUNK):
        cols = slice(c * COL_CHUNK, (c + 1) * COL_CHUNK)
        a = jnp.maximum(_dot(h, w1_ref[:, cols]), 0.0)
        acc = acc + _dot((a * a).astype(BF16), w2_ref[cols, :])
    o_ref[...] = acc
    for src, dst in zip(cast_src, cast_dst):
        dst[...] = src[0].astype(BF16)


def _can_cast_in_steps(weights, steps):
    return all(stack.shape[1] % steps == 0 and (stack.shape[1] // steps) % BF16_SUBLANES == 0
               for stack, _ in weights)


def _out_mlp(x, y, wo, gain, w1, w2, next_weights=()):
    n = x.shape[0]
    steps = n // TOKEN_TILE
    tile_spec = pl.BlockSpec((TOKEN_TILE, D_MODEL), lambda i: (i, 0))
    slab_in = [pl.BlockSpec((1, stack.shape[1] // steps, stack.shape[2]), lambda i, _l=idx: (_l, i, 0))
               for stack, idx in next_weights]
    slab_out = [pl.BlockSpec((stack.shape[1] // steps, stack.shape[2]), lambda i: (i, 0))
                for stack, _ in next_weights]
    outs = pl.pallas_call(
        functools.partial(_out_mlp_kernel, n_cast=len(next_weights)),
        grid=(steps,),
        in_specs=[
            tile_spec,
            tile_spec,
            _const_spec((D_MODEL, D_MODEL)),
            _const_spec((1, D_MODEL)),
            _const_spec((D_MODEL, D_FF)),
            _const_spec((D_FF, D_MODEL)),
        ] + slab_in,
        out_specs=[tile_spec] + slab_out,
        out_shape=[jax.ShapeDtypeStruct((n, D_MODEL), F32)]
        + [jax.ShapeDtypeStruct(stack.shape[1:], BF16) for stack, _ in next_weights],
        compiler_params=pltpu.CompilerParams(
            dimension_semantics=("arbitrary",), vmem_limit_bytes=VMEM_LIMIT_BYTES),
        name="out_mlp",
    )(x, y, wo, gain, w1, w2, *[stack for stack, _ in next_weights])
    return outs[0], list(outs[1:])


def kernel(x, norm_gains, sb_w_qkv, sb_q_gain, sb_k_gain, sb_w_o, hg_w_in, hg_lb_logits, hg_norm_gain, hg_w_o, mlp_w1, mlp_w2):
    bsz, seq, d = x.shape
    depth = norm_gains.shape[0]
    n = bsz * seq
    assert d == D_MODEL and n % TOKEN_TILE == 0 and seq % 512 == 0
    xf = x.reshape(n, d)

    def layer_weights(layer):
        j = layer // 2
        if layer % 2 == 0:
            return [(sb_w_qkv, j), (sb_w_o, j), (mlp_w1, layer), (mlp_w2, layer)]
        return [(hg_w_in, j), (hg_w_o, j), (mlp_w1, layer), (mlp_w2, layer)]

    w_in, w_out, w1, w2 = [stack[idx].astype(BF16) for stack, idx in layer_weights(0)]
    for layer in range(depth):
        j = layer // 2
        g_mix = norm_gains[layer, 0].reshape(1, d)
        g_mlp = norm_gains[layer, 1].reshape(1, d)
        if layer % 2 == 0:
            qkv = _sb_proj(xf, g_mix, w_in)
            y = _sb_attn(qkv.reshape(bsz, seq, 3 * d),
                         jnp.tile(sb_q_gain[j], 2).reshape(1, LANES),
                         jnp.tile(sb_k_gain[j], 2).reshape(1, LANES))
        else:
            q, lf, iv, gs = _hg_proj(xf, g_mix, w_in, hg_lb_logits, j)
            shp = (bsz, seq, d)
            y = _hg_scan(q.reshape(shp), lf.reshape(shp), iv.reshape(shp), gs.reshape(shp),
                         hg_norm_gain[j].reshape(1, HG_HEAD_DIM))
        nxt = layer_weights(layer + 1) if layer + 1 < depth else []
        in_kernel = _can_cast_in_steps(nxt, n // TOKEN_TILE)
        xf, cast = _out_mlp(xf, y.reshape(n, d), w_out, g_mlp, w1, w2, nxt if in_kernel else ())
        if nxt:
            w_in, w_out, w1, w2 = cast if in_kernel else [stack[idx].astype(BF16) for stack, idx in nxt]
    return xf.reshape(bsz, seq, d)
```
